```python
import math
import jax, jax.numpy as jnp
from jax import lax
import numpy as np


D_MODEL = 1024
BATCH = 4
SEQ = 8192
DEPTH = 2
DEC_BATCH = 2
DEC_SEQ = 8192
PAST_LEN = 128

GRID_W = 64
PLE_DIM = 256
EPS = 1e-6
N_EVEN = (DEPTH + 1) // 2
N_ODD = DEPTH // 2

NA_HEADS = 8
NA_HEAD_DIM = 64
NA_WIN_ROWS_MAX = 8
NA_WIN_COLS = 16
NA_WIDTH = NA_HEADS * NA_HEAD_DIM
ML_HEADS = 4
ML_HEAD_DIM = 128
ML_WIDTH = ML_HEADS * ML_HEAD_DIM
ML_CHUNK = 64
ML_FORGET_BIAS = 3.0
EVEN_SPLITS = [NA_WIDTH, 2 * NA_WIDTH, 3 * NA_WIDTH, 3 * NA_WIDTH + ML_WIDTH,
               3 * NA_WIDTH + 2 * ML_WIDTH, 3 * NA_WIDTH + 3 * ML_WIDTH, 3 * NA_WIDTH + 4 * ML_WIDTH]
EVEN_IN = 3 * NA_WIDTH + 4 * ML_WIDTH + 4 * ML_HEADS
GLA_HEADS = 4
GLA_DK = 128
GLA_DV = 256
GLA_KW = GLA_HEADS * GLA_DK
GLA_VW = GLA_HEADS * GLA_DV
GLA_CHUNK = 64
GLA_GATE_RANK = 16
GLA_TAU = 16.0
ODD_SPLITS = [GLA_KW, 2 * GLA_KW, 2 * GLA_KW + GLA_VW, 2 * GLA_KW + 2 * GLA_VW,
              2 * GLA_KW + 2 * GLA_VW + GLA_GATE_RANK]
ODD_IN = 2 * GLA_KW + 2 * GLA_VW + 2 * GLA_GATE_RANK
D_FF = 2816
CONV_W = 3

kernel_name = 'hybrid_natten_mlstm_gla_encoder'


def rmsnorm(x, g):
    x32 = x.astype(jnp.float32)
    y = x32 * lax.rsqrt(jnp.mean(x32 * x32, axis=-1, keepdims=True) + EPS)
    return (y * g).astype(x.dtype)


def head_rmsnorm(y, n_heads, g):
    shp = y.shape
    y32 = y.astype(jnp.float32).reshape(*shp[:-1], n_heads, -1)
    y32 = y32 * lax.rsqrt(jnp.mean(y32 * y32, axis=-1, keepdims=True) + EPS)
    return (y32.reshape(shp) * g).astype(y.dtype)


def to_chunks(a, L):
    Bn, S, H = a.shape[:3]
    a = a.reshape(Bn, S // L, L, H, *a.shape[3:])
    return jnp.moveaxis(a, (1, 3), (0, 2))


def from_chunks(a):
    a = jnp.moveaxis(a, (0, 2), (1, 3))
    Bn, nc, L, H, d = a.shape
    return a.reshape(Bn, nc * L, H, d)


def neighborhood_attention(q, k, v, rpb):
    Bn, S, H, dh = q.shape
    rows = S // GRID_W
    kh = min(NA_WIN_ROWS_MAX, rows)
    kw = NA_WIN_COLS
    qg = q.reshape(Bn, rows, GRID_W, H, dh) * (dh ** -0.5)
    kg = k.reshape(Bn, rows, GRID_W, H, dh)
    vg = v.reshape(Bn, rows, GRID_W, H, dh)
    cols = np.arange(GRID_W)
    col_start = np.clip(cols - kw // 2, 0, GRID_W - kw)
    col_idx = col_start[:, None] + np.arange(kw)[None, :]
    dc = col_idx - cols[:, None] + (NA_WIN_COLS - 1)
    rpb_c = rpb[:, :, dc]

    def one_row(r):
        rs = jnp.clip(r - kh // 2, 0, rows - kh)
        kb = lax.dynamic_slice_in_dim(kg, rs, kh, axis=1)[:, :, col_idx]
        vb = lax.dynamic_slice_in_dim(vg, rs, kh, axis=1)[:, :, col_idx]
        qr = lax.dynamic_index_in_dim(qg, r, axis=1, keepdims=False)
        dr = rs + jnp.arange(kh) - r + (NA_WIN_ROWS_MAX - 1)
        bias = jnp.transpose(rpb_c[:, dr], (0, 2, 1, 3)).astype(jnp.float32)
        s = jnp.einsum('bchd,bkcjhd->bhckj', qr, kb).astype(jnp.float32) + bias[None]
        pr = jax.nn.softmax(s.reshape(Bn, H, GRID_W, kh * kw), axis=-1)
        pr = pr.reshape(Bn, H, GRID_W, kh, kw).astype(v.dtype)
        return jnp.einsum('bhckj,bkcjhd->bchd', pr, vb)

    out = lax.map(one_row, jnp.arange(rows))
    return jnp.moveaxis(out, 0, 1).reshape(Bn, S, H * dh)


def mlstm_chunked(q, k, v, log_i, log_f):
    Bn, S, H, d = q.shape
    L = ML_CHUNK
    qc = to_chunks(q.astype(jnp.float32), L)
    kc = to_chunks(k.astype(jnp.float32) * (d ** -0.5), L)
    vc = to_chunks(v.astype(jnp.float32), L)
    li = to_chunks(log_i, L)
    lf = to_chunks(log_f, L)
    mask = jnp.tril(jnp.ones((L, L), dtype=bool))

    def step(carry, inp):
        C, n, m = carry
        qx, kx, vx, lix, lfx = inp
        b = jnp.cumsum(lfx, axis=-1)
        Dm = b[..., :, None] - b[..., None, :] + lix[..., None, :]
        Dm = jnp.where(mask, Dm, -jnp.inf)
        inter = b + m[..., None]
        mt = jnp.maximum(inter, jnp.max(Dm, axis=-1))
        sc = jnp.einsum('bhtd,bhsd->bhts', qx, kx) * jnp.exp(Dm - mt[..., None])
        ai = jnp.exp(inter - mt)
        num = jnp.einsum('bhts,bhsd->bhtd', sc, vx) + ai[..., None] * jnp.einsum('bhvk,bhtk->bhtv', C, qx)
        den = jnp.sum(sc, axis=-1) + ai * jnp.einsum('bhk,bhtk->bht', n, qx)
        h = num / jnp.maximum(jnp.abs(den), jnp.exp(-mt))[..., None]
        bl = b[..., -1]
        ds = bl[..., None] - b + lix
        m_new = jnp.maximum(bl + m, jnp.max(ds, axis=-1))
        decay = jnp.exp(bl + m - m_new)
        ws = jnp.exp(ds - m_new[..., None])
        C_new = decay[..., None, None] * C + jnp.einsum('bhs,bhsv,bhsk->bhvk', ws, vx, kx)
        n_new = decay[..., None] * n + jnp.einsum('bhs,bhsk->bhk', ws, kx)
        return (C_new, n_new, m_new), h

    init = (jnp.zeros((Bn, H, d, d), jnp.float32), jnp.zeros((Bn, H, d), jnp.float32),
            jnp.zeros((Bn, H), jnp.float32))
    _, hs = lax.scan(step, init, (qc, kc, vc, li, lf))
    return from_chunks(hs)


def gla_chunked(q, k, v, log_a):
    Bn, S, H, dk = q.shape
    dv = v.shape[-1]
    L = GLA_CHUNK
    qc = to_chunks(q.astype(jnp.float32) * (dk ** -0.5), L)
    kc = to_chunks(k.astype(jnp.float32), L)
    vc = to_chunks(v.astype(jnp.float32), L)
    ac = to_chunks(log_a, L)
    mask = jnp.tril(jnp.ones((L, L), dtype=bool))[..., None]

    def step(St, inp):
        qx, kx, vx, lax_ = inp
        b = jnp.cumsum(lax_, axis=-2)
        diff = b[..., :, None, :] - b[..., None, :, :]
        dec = jnp.exp(jnp.where(mask, diff, -jnp.inf))
        A = jnp.einsum('bhtk,bhtsk,bhsk->bhts', qx, dec, kx)
        o = jnp.einsum('bhts,bhsv->bhtv', A, vx) + jnp.einsum('bhtk,bhkv->bhtv', qx * jnp.exp(b), St)
        bl = b[..., -1:, :]
        S_new = jnp.exp(bl[..., 0, :])[..., None] * St + jnp.einsum('bhsk,bhsv->bhkv', kx * jnp.exp(bl - b), vx)
        return S_new, o

    _, os_ = lax.scan(step, jnp.zeros((Bn, H, dk, dv), jnp.float32), (qc, kc, vc, ac))
    return from_chunks(os_)


def flip(a):
    return jnp.flip(a, axis=1)


def even_mixer(h, w_in, rpb, ml_gate_b, ml_norm_g, w_out):
    Bn, S, _ = h.shape
    z = h @ w_in
    na_q, na_k, na_v, ml_q, ml_k, ml_v, ml_o, ml_g = jnp.split(z, EVEN_SPLITS, axis=-1)
    hd = (Bn, S, NA_HEADS, NA_HEAD_DIM)
    y_na = neighborhood_attention(na_q.reshape(hd), na_k.reshape(hd), na_v.reshape(hd), rpb)
    md = (Bn, S, ML_HEADS, ML_HEAD_DIM)
    mq, mk, mv = ml_q.reshape(md), ml_k.reshape(md), ml_v.reshape(md)
    gates = (ml_g + ml_gate_b).astype(jnp.float32).reshape(Bn, S, 4, ML_HEADS)
    li_f, lf_f = gates[:, :, 0], jax.nn.log_sigmoid(gates[:, :, 1])
    li_b, lf_b = gates[:, :, 2], jax.nn.log_sigmoid(gates[:, :, 3])
    h_f = mlstm_chunked(mq, mk, mv, li_f, lf_f)
    h_b = flip(mlstm_chunked(flip(mq), flip(mk), flip(mv), flip(li_b), flip(lf_b)))
    h_ml = (h_f + h_b).reshape(Bn, S, ML_WIDTH).astype(h.dtype)
    y_ml = jax.nn.sigmoid(ml_o) * head_rmsnorm(h_ml, ML_HEADS, ml_norm_g)
    return jnp.concatenate([y_na.astype(h.dtype), y_ml], axis=-1) @ w_out


def odd_mixer(h, w_in, gate_up, gate_b, norm_g, w_out):
    Bn, S, _ = h.shape
    z = h @ w_in
    q, k, v, r, gd_f, gd_b = jnp.split(z, ODD_SPLITS, axis=-1)
    kd = (Bn, S, GLA_HEADS, GLA_DK)
    q, k = q.reshape(kd), k.reshape(kd)
    v = v.reshape(Bn, S, GLA_HEADS, GLA_DV)
    la_f = (jax.nn.log_sigmoid((gd_f @ gate_up[0] + gate_b[0]).astype(jnp.float32)) / GLA_TAU).reshape(kd)
    la_b = (jax.nn.log_sigmoid((gd_b @ gate_up[1] + gate_b[1]).astype(jnp.float32)) / GLA_TAU).reshape(kd)
    o_f = gla_chunked(q, k, v, la_f)
    o_b = flip(gla_chunked(flip(q), flip(k), flip(v), flip(la_b)))
    o = (o_f + o_b).reshape(Bn, S, GLA_VW).astype(h.dtype)
    y = head_rmsnorm(o, GLA_HEADS, norm_g) * jax.nn.silu(r)
    return y @ w_out


def conv_ffn(h, w_up, conv_w, conv_b, w_down):
    u = h @ w_up
    up = jnp.pad(u, ((0, 0), (1, 1), (0, 0)))
    c = up[:, :-2] * conv_w[0] + up[:, 1:-1] * conv_w[1] + up[:, 2:] * conv_w[2] + conv_b
    a, g = jnp.split(c, 2, axis=-1)
    return (jax.nn.gelu(g) * a) @ w_down


def trunk(x, p, norm_mix_pre, norm_mix_post, norm_ffn_pre, norm_ffn_post,
          even_w_in, even_na_rpb, even_ml_gate_b, even_ml_norm_g, even_w_out,
          odd_w_in, odd_gate_up, odd_gate_b, odd_norm_g, odd_w_out,
          ffn_w_up, ffn_conv_w, ffn_conv_b, ffn_w_down, ple_w_proj, ple_w_gate):
    for i in range(DEPTH):
        j = i // 2
        hn = rmsnorm(x, norm_mix_pre[i])
        if i % 2 == 0:
            m = even_mixer(hn, even_w_in[j], even_na_rpb[j], even_ml_gate_b[j], even_ml_norm_g[j], even_w_out[j])
        else:
            m = odd_mixer(hn, odd_w_in[j], odd_gate_up[j], odd_gate_b[j], odd_norm_g[j], odd_w_out[j])
        x = x + rmsnorm(m, norm_mix_post[i])
        hn = rmsnorm(x, norm_ffn_pre[i])
        x = x + rmsnorm(conv_ffn(hn, ffn_w_up[i], ffn_conv_w[i], ffn_conv_b[i], ffn_w_down[i]), norm_ffn_post[i])
        gate = jax.nn.sigmoid(x @ ple_w_gate[i])
        x = x + (p[i] @ ple_w_proj[i]) * gate
    return x


def setup_inputs(seed: int = 0) -> dict:
    key = jax.random.key(seed)
    ks = iter(jax.random.split(key, 32))

    def nrm(shape, scale):
        return jax.random.normal(next(ks), shape, jnp.float32) * scale

    def gain(shape):
        return 1.0 + nrm(shape, 0.05)

    ne, no = N_EVEN, N_ODD
    if_offset = jnp.repeat(jnp.array([0.0, ML_FORGET_BIAS, 0.0, ML_FORGET_BIAS], jnp.float32), ML_HEADS)
    d = {}
    d['x_prompt'] = nrm((BATCH, SEQ, D_MODEL), 1.0)
    d['x_sample'] = nrm((DEC_BATCH, DEC_SEQ, D_MODEL), 1.0)
    d['p_prompt'] = nrm((DEPTH, BATCH, SEQ, PLE_DIM), 1.0)
    d['p_sample'] = nrm((DEPTH, DEC_BATCH, DEC_SEQ, PLE_DIM), 1.0)
    d['norm_mix_pre'] = gain((DEPTH, D_MODEL))
    d['norm_mix_post'] = gain((DEPTH, D_MODEL))
    d['norm_ffn_pre'] = gain((DEPTH, D_MODEL))
    d['norm_ffn_post'] = gain((DEPTH, D_MODEL))
    d['even_w_in'] = nrm((ne, D_MODEL, EVEN_IN), D_MODEL ** -0.5)
    d['even_na_rpb'] = nrm((ne, NA_HEADS, 2 * NA_WIN_ROWS_MAX - 1, 2 * NA_WIN_COLS - 1), 0.1)
    d['even_ml_gate_b'] = nrm((ne, 4 * ML_HEADS), 0.1) + if_offset
    d['even_ml_norm_g'] = gain((ne, ML_WIDTH))
    d['even_w_out'] = nrm((ne, NA_WIDTH + ML_WIDTH, D_MODEL), (NA_WIDTH + ML_WIDTH) ** -0.5)
    d['odd_w_in'] = nrm((no, D_MODEL, ODD_IN), D_MODEL ** -0.5)
    d['odd_gate_up'] = nrm((no, 2, GLA_GATE_RANK, GLA_KW), GLA_GATE_RANK ** -0.5)
    d['odd_gate_b'] = nrm((no, 2, GLA_KW), 0.1)
    d['odd_norm_g'] = gain((no, GLA_VW))
    d['odd_w_out'] = nrm((no, GLA_VW, D_MODEL), GLA_VW ** -0.5)
    d['ffn_w_up'] = nrm((DEPTH, D_MODEL, 2 * D_FF), D_MODEL ** -0.5)
    d['ffn_conv_w'] = nrm((DEPTH, CONV_W, 2 * D_FF), CONV_W ** -0.5)
    d['ffn_conv_b'] = nrm((DEPTH, 2 * D_FF), 0.02)
    d['ffn_w_down'] = nrm((DEPTH, D_FF, D_MODEL), D_FF ** -0.5)
    d['ple_w_proj'] = nrm((DEPTH, PLE_DIM, D_MODEL), PLE_DIM ** -0.5)
    d['ple_w_gate'] = nrm((DEPTH, D_MODEL, D_MODEL), D_MODEL ** -0.5)
    return d


def reference(x_prompt, x_sample, p_prompt, p_sample, norm_mix_pre, norm_mix_post, norm_ffn_pre, norm_ffn_post,
              even_w_in, even_na_rpb, even_ml_gate_b, even_ml_norm_g, even_w_out,
              odd_w_in, odd_gate_up, odd_gate_b, odd_norm_g, odd_w_out,
              ffn_w_up, ffn_conv_w, ffn_conv_b, ffn_w_down, ple_w_proj, ple_w_gate):
    y_prompt = trunk(x_prompt, p_prompt, norm_mix_pre, norm_mix_post, norm_ffn_pre, norm_ffn_post,
                     even_w_in, even_na_rpb, even_ml_gate_b, even_ml_norm_g, even_w_out,
                     odd_w_in, odd_gate_up, odd_gate_b, odd_norm_g, odd_w_out,
                     ffn_w_up, ffn_conv_w, ffn_conv_b, ffn_w_down, ple_w_proj, ple_w_gate)
    y_sample = trunk(x_sample, p_sample, norm_mix_pre, norm_mix_post, norm_ffn_pre, norm_ffn_post,
                     even_w_in, even_na_rpb, even_ml_gate_b, even_ml_norm_g, even_w_out,
                     odd_w_in, odd_gate_up, odd_gate_b, odd_norm_g, odd_w_out,
                     ffn_w_up, ffn_conv_w, ffn_conv_b, ffn_w_down, ple_w_proj, ple_w_gate)
    return (y_prompt, y_sample)
```

```python
import functools

import jax
import jax.numpy as jnp
import numpy as np
from jax import lax
from jax.experimental import pallas as pl
from jax.experimental.pallas import tpu as pltpu

f32 = jnp.float32
bf16 = jnp.bfloat16

D_MODEL = 1024
EPS = 1e-6
GRID_W = 64
NA_HEADS = 8
NA_HEAD_DIM = 64
NA_WIN_ROWS = 8
NA_WIN_COLS = 16
NA_WIDTH = NA_HEADS * NA_HEAD_DIM
ML_HEADS = 4
ML_HEAD_DIM = 128
ML_WIDTH = ML_HEADS * ML_HEAD_DIM
GLA_HEADS = 4
GLA_DK = 128
GLA_DV = 256
GLA_KW = GLA_HEADS * GLA_DK
GLA_VW = GLA_HEADS * GLA_DV
GLA_GATE_RANK = 16
GLA_TAU = 16.0
CHUNK = 64
D_FF = 2816
LANE = 128
NEG = -1e30
GLA_LEVELS = 6

VMEM_LIMIT = 56 * 1024 * 1024


def _cparams(sem):
    return pltpu.CompilerParams(dimension_semantics=sem, vmem_limit_bytes=VMEM_LIMIT)


def _dot(a, b):
    return jnp.dot(a, b, preferred_element_type=f32)


def _dot_nt(a, b):
    return lax.dot_general(a, b, (((1,), (1,)), ((), ())), preferred_element_type=f32)


def _dot_tn(a, b):
    return lax.dot_general(a, b, (((0,), (0,)), ((), ())), preferred_element_type=f32)


def _dot_f32(a, b):
    return jnp.dot(a, b, precision=lax.Precision.HIGHEST, preferred_element_type=f32)


def _rms(x):
    return x * lax.rsqrt(jnp.mean(x * x, axis=-1, keepdims=True) + EPS)


def _sigmoid(x):
    return 1.0 / (1.0 + jnp.exp(-x))


def _log_sigmoid(x):
    return jnp.minimum(x, 0.0) - jnp.log1p(jnp.exp(-jnp.abs(x)))


def _gelu_tanh(x):
    return 0.5 * x * (1.0 + jnp.tanh(0.7978845608028654 * (x + 0.044715 * (x * x * x))))


def _norm_matmul_body(x_ref, g_ref, w_ref, *o_refs, widths):
    hn = (_rms(x_ref[...]) * g_ref[...]).astype(bf16)
    off = 0
    for o_ref, n in zip(o_refs, widths):
        for c0 in range(0, n, 512):
            cw = min(512, n - c0)
            o_ref[:, c0:c0 + cw] = _dot(hn, w_ref[:, off + c0:off + c0 + cw]).astype(o_ref.dtype)
        off += n


def _norm_matmul(x2d, g, w, widths, dtypes, tm=512):
    T, Dm = x2d.shape
    N = w.shape[1]
    assert sum(widths) == N and T % tm == 0
    return pl.pallas_call(
        functools.partial(_norm_matmul_body, widths=tuple(widths)),
        grid=(T // tm,),
        in_specs=[pl.BlockSpec((tm, Dm), lambda i: (i, 0)),
                  pl.BlockSpec((1, Dm), lambda i: (0, 0)),
                  pl.BlockSpec((Dm, N), lambda i: (0, 0))],
        out_specs=[pl.BlockSpec((tm, n), lambda i: (i, 0)) for n in widths],
        out_shape=[jax.ShapeDtypeStruct((T, n), dt) for n, dt in zip(widths, dtypes)],
        compiler_params=_cparams(("parallel",)),
        name="norm_matmul",
    )(x2d, g.reshape(1, Dm), w)


def _na_body(q_ref, k_ref, v_ref, bias_ref, o_ref, *, R, rows):
    i = pl.program_id(2)
    lane = lax.broadcasted_iota(jnp.int32, (GRID_W, LANE), 1)
    first = lane < NA_HEAD_DIM
    nkeys = NA_WIN_ROWS * GRID_W

    def row_body(j, carry):
        r = i * R + j
        rs = jnp.clip(r - NA_WIN_ROWS // 2, 0, rows - NA_WIN_ROWS)
        case = r - rs
        q = q_ref[0, pl.ds(pl.multiple_of(j * GRID_W, GRID_W), GRID_W), :]
        kstart = pl.multiple_of(rs * GRID_W, GRID_W)
        kw = k_ref[0, pl.ds(kstart, nkeys), :]
        vw = v_ref[0, pl.ds(kstart, nkeys), :]
        zero = jnp.zeros_like(q)
        q2 = jnp.concatenate([jnp.where(first, q, zero), jnp.where(first, zero, q)], axis=0)
        s = _dot_nt(q2, kw) * (NA_HEAD_DIM ** -0.5)
        s = s + jnp.concatenate([bias_ref[case, 0], bias_ref[case, 1]], axis=0)
        m = jnp.max(s, axis=-1, keepdims=True)
        p = jnp.exp(s - m)
        l = jnp.sum(p, axis=-1, keepdims=True)
        o = _dot(p.astype(bf16), vw) / l
        o_ref[0, pl.ds(pl.multiple_of(j * GRID_W, GRID_W), GRID_W), :] = jnp.where(
            first, o[:GRID_W], o[GRID_W:]).astype(o_ref.dtype)
        return carry

    lax.fori_loop(0, R, row_body, 0)


def _na_bias_table(rpb):
    cols = np.arange(GRID_W)
    cs = np.clip(cols - NA_WIN_COLS // 2, 0, GRID_W - NA_WIN_COLS)
    kc = np.arange(GRID_W)
    valid = (kc[None, :] >= cs[:, None]) & (kc[None, :] < cs[:, None] + NA_WIN_COLS)
    dc = np.clip(kc[None, :] - cols[:, None] + (NA_WIN_COLS - 1), 0, 2 * NA_WIN_COLS - 2)
    off = np.arange(NA_WIN_ROWS)
    dr = np.arange(NA_WIN_ROWS)[None, :] - off[:, None] + (NA_WIN_ROWS - 1)
    t = rpb[:, dr][:, :, :, dc]
    t = jnp.where(valid[None, None, None], t.astype(f32), NEG)
    t = jnp.transpose(t, (1, 0, 3, 2, 4))
    return t.reshape(NA_WIN_ROWS, NA_HEADS, GRID_W, NA_WIN_ROWS * GRID_W)


def _neighborhood_attention(a3, rpb, R=8):
    B, S, _ = a3.shape
    rows = S // GRID_W
    assert rows >= NA_WIN_ROWS and rows % R == 0
    bias = _na_bias_table(rpb)
    npair = NA_WIDTH // LANE
    return pl.pallas_call(
        functools.partial(_na_body, R=R, rows=rows),
        grid=(B, npair, rows // R),
        in_specs=[pl.BlockSpec((1, R * GRID_W, LANE), lambda b, p, i: (b, i, p)),
                  pl.BlockSpec((1, S, LANE), lambda b, p, i: (b, 0, npair + p)),
                  pl.BlockSpec((1, S, LANE), lambda b, p, i: (b, 0, 2 * npair + p)),
                  pl.BlockSpec((NA_WIN_ROWS, 2, GRID_W, NA_WIN_ROWS * GRID_W), lambda b, p, i: (0, p, 0, 0))],
        out_specs=pl.BlockSpec((1, R * GRID_W, LANE), lambda b, p, i: (b, i, p)),
        out_shape=jax.ShapeDtypeStruct((B, S, NA_WIDTH), bf16),
        compiler_params=_cparams(("parallel", "parallel", "arbitrary")),
        name="neighborhood_attention",
    )(a3, a3, a3, bias)


def _mlstm_body(qf, kf, vf, qb, kb, vb, gf, gb, gtf, gtb, brow, bcol, hf_ref, hb_ref,
                c_ref, n_ref, m_ref, *, nch):
    L = CHUNK
    H = ML_HEADS
    dh = ML_HEAD_DIM
    scale = dh ** -0.5

    @pl.when(pl.program_id(1) == 0)
    def _():
        c_ref[...] = jnp.zeros_like(c_ref)
        n_ref[...] = jnp.zeros_like(n_ref)
        m_ref[...] = jnp.zeros_like(m_ref)

    ri = lax.broadcasted_iota(jnp.int32, (L, L), 0)
    ci = lax.broadcasted_iota(jnp.int32, (L, L), 1)
    tril = ci <= ri
    triu = ci >= ri
    trilf = tril.astype(f32)
    triuf = triu.astype(f32)
    lane = lax.broadcasted_iota(jnp.int32, (1, LANE), 1)
    f_col = ((lane // H) % 2 == 1) & (lane < 4 * H)
    sub = lax.broadcasted_iota(jnp.int32, (4 * H, 1), 0)
    f_row = (sub // H) % 2 == 1

    def chunk(j, carry):
        for d in range(2):
            cj = j if d == 0 else nch - 1 - j
            st = pl.multiple_of(cj * L, L)
            q_ref, k_ref, v_ref, g_ref, gt_ref, h_ref = (
                (qf, kf, vf, gf, gtf, hf_ref) if d == 0 else (qb, kb, vb, gb, gtb, hb_ref))
            G = g_ref[0, pl.ds(st, L), :] + brow[...]
            G = jnp.where(f_col, _log_sigmoid(G), G)
            GT = gt_ref[0, cj] + bcol[...]
            GT = jnp.where(f_row, _log_sigmoid(GT), GT)
            if d == 0:
                cum_col = _dot_f32(trilf, G)
                cum_row = _dot_f32(GT, triuf)
                mask = tril
            else:
                cum_col = _dot_f32(triuf, G)
                cum_row = _dot_f32(GT, trilf)
                mask = triu
            for h in range(H):
                u = d * H + h
                c_i = d * 2 * H + h
                c_f = c_i + H
                hs = slice(h * dh, (h + 1) * dh)
                li_col = G[:, c_i:c_i + 1]
                li_row = GT[c_i:c_i + 1, :]
                b_col = cum_col[:, c_f:c_f + 1]
                b_row = cum_row[c_f:c_f + 1, :]
                bl = b_row[:, L - 1:L] if d == 0 else b_row[:, 0:1]
                q = q_ref[0, pl.ds(st, L), hs]
                k = k_ref[0, pl.ds(st, L), hs]
                v = v_ref[0, pl.ds(st, L), hs]
                ct = c_ref[u]
                nrow = n_ref[u]
                m_prev = m_ref[u][:, 0:1]

                dm = jnp.where(mask, b_col - b_row + li_row, NEG)
                inter = b_col + m_prev
                mt = jnp.maximum(inter, jnp.max(dm, axis=-1, keepdims=True))
                sc = _dot_nt(q, k) * scale * jnp.exp(dm - mt)
                ais = jnp.exp(inter - mt) * scale
                num = _dot(sc.astype(bf16), v) + ais * _dot(q, ct.astype(bf16))
                qn = jnp.sum(q.astype(f32) * nrow, axis=-1, keepdims=True)
                den = jnp.sum(sc, axis=-1, keepdims=True) + ais * qn
                h_ref[0, pl.ds(st, L), hs] = num / jnp.maximum(jnp.abs(den), jnp.exp(-mt))

                ds_row = bl - b_row + li_row
                m_new = jnp.maximum(bl + m_prev, jnp.max(ds_row, axis=-1, keepdims=True))
                decay = jnp.exp(bl + m_prev - m_new)
                ws_col = jnp.exp(bl - b_col + li_col - m_new)
                wv = (ws_col * v.astype(f32)).astype(bf16)
                c_ref[u] = decay * ct + _dot_tn(k, wv)
                n_ref[u] = decay * nrow + jnp.sum(ws_col * k.astype(f32), axis=0, keepdims=True)
                m_ref[u] = jnp.broadcast_to(m_new, (1, LANE))
        return carry

    lax.fori_loop(0, nch, chunk, 0)


def _mlstm(m4, g, gate_b, tb=512):
    B, S, _ = m4.shape
    tb = min(tb, S)
    assert S % tb == 0 and tb % CHUNK == 0
    nb = S // tb
    nch = tb // CHUNK
    ng = 4 * ML_HEADS
    gt = jnp.swapaxes(g[..., :ng].reshape(B, S // CHUNK, CHUNK, ng), 2, 3)
    brow = jnp.zeros((1, LANE), f32).at[0, :ng].set(gate_b.astype(f32))
    bcol = gate_b.astype(f32).reshape(ng, 1)
    W = ML_WIDTH

    def fwd(c):
        return lambda b, i: (b, i, c)

    def bwd(c):
        return lambda b, i: (b, nb - 1 - i, c)

    in_specs = (
        [pl.BlockSpec((1, tb, W), fwd(c)) for c in range(3)]
        + [pl.BlockSpec((1, tb, W), bwd(c)) for c in range(3)]
        + [pl.BlockSpec((1, tb, LANE), fwd(0)), pl.BlockSpec((1, tb, LANE), bwd(0)),
           pl.BlockSpec((1, nch, ng, CHUNK), lambda b, i: (b, i, 0, 0)),
           pl.BlockSpec((1, nch, ng, CHUNK), lambda b, i: (b, nb - 1 - i, 0, 0)),
           pl.BlockSpec((1, LANE), lambda b, i: (0, 0)),
           pl.BlockSpec((ng, 1), lambda b, i: (0, 0))])
    return pl.pallas_call(
        functools.partial(_mlstm_body, nch=nch),
        grid=(B, nb),
        in_specs=in_specs,
        out_specs=[pl.BlockSpec((1, tb, W), fwd(0)), pl.BlockSpec((1, tb, W), bwd(0))],
        out_shape=[jax.ShapeDtypeStruct((B, S, W), f32)] * 2,
        scratch_shapes=[pltpu.VMEM((2 * ML_HEADS, ML_HEAD_DIM, ML_HEAD_DIM), f32),
                        pltpu.VMEM((2 * ML_HEADS, 1, ML_HEAD_DIM), f32),
                        pltpu.VMEM((2 * ML_HEADS, 1, LANE), f32)],
        compiler_params=_cparams(("parallel", "arbitrary")),
        name="mlstm",
    )(m4, m4, m4, m4, m4, m4, g, g, gt, gt, brow, bcol)


def _even_out_body(x_ref, na_ref, hf_ref, hb_ref, o_ref, g_ref, w_ref, gp_ref, out_ref):
    hm = hf_ref[...] + hb_ref[...]
    hn = jnp.concatenate(
        [_rms(hm[:, h * ML_HEAD_DIM:(h + 1) * ML_HEAD_DIM]) for h in range(ML_HEADS)], axis=-1)
    y_ml = _sigmoid(o_ref[...].astype(f32)) * (hn * g_ref[...])
    m = _dot(na_ref[...], w_ref[:NA_WIDTH, :]) + _dot(y_ml.astype(bf16), w_ref[NA_WIDTH:, :])
    out_ref[...] = x_ref[...] + _rms(m) * gp_ref[...]


def _even_out(x2d, na2d, hf2d, hb2d, m4_2d, ml_g, w_out, g_post, tm=512):
    T, Dm = x2d.shape
    W = ML_WIDTH
    row = lambda i: (i, 0)
    const = lambda i: (0, 0)
    return pl.pallas_call(
        _even_out_body,
        grid=(T // tm,),
        in_specs=[pl.BlockSpec((tm, Dm), row), pl.BlockSpec((tm, NA_WIDTH), row),
                  pl.BlockSpec((tm, W), row), pl.BlockSpec((tm, W), row),
                  pl.BlockSpec((tm, W), lambda i: (i, 3)),
                  pl.BlockSpec((1, W), const), pl.BlockSpec((NA_WIDTH + W, Dm), const),
                  pl.BlockSpec((1, Dm), const)],
        out_specs=pl.BlockSpec((tm, Dm), row),
        out_shape=jax.ShapeDtypeStruct((T, Dm), f32),
        compiler_params=_cparams(("parallel",)),
        name="even_out",
    )(x2d, na2d, hf2d, hb2d, m4_2d, ml_g.reshape(1, W), w_out, g_post.reshape(1, Dm))


def _gla_tables():
    L = CHUNK
    t = np.arange(L)
    cm = np.zeros((1 + GLA_LEVELS, L, L), np.float32)
    pm = np.zeros((GLA_LEVELS + 1, L, L), np.float32)
    cm[0] = (t[None, :] <= t[:, None])
    for lv in range(GLA_LEVELS):
        g = (L // 2) >> lv
        a = (t // (2 * g)) * 2 * g
        upper = (t - a) >= g
        u = t[None, :]
        q_rows = (u >= (a + g)[:, None]) & (u <= t[:, None])
        k_rows = (u > t[:, None]) & (u <= (a + g - 1)[:, None])
        cm[1 + lv] = np.where(upper[:, None], q_rows, k_rows)
        pm[lv] = (a[:, None] == a[None, :]) & upper[:, None] & (~upper)[None, :]
    pm[GLA_LEVELS] = np.eye(L)
    cm_b = cm[:, ::-1, ::-1]
    pm_b = pm[:, ::-1, ::-1]
    cm2 = np.stack([cm.reshape(-1, L), cm_b.reshape(-1, L)])
    pm2 = np.stack([pm, pm_b])
    return cm2, pm2


def _gla_body(qf, kf, vf, gdf, qb, kb, vb, gdb, gu_ref, gbias_ref, cm_ref, pm_ref,
              of_ref, ob_ref, s_ref, *, nch):
    L = CHUNK
    H = GLA_HEADS
    dk = GLA_DK
    dv = GLA_DV
    scale = dk ** -0.5

    @pl.when(pl.program_id(1) == 0)
    def _():
        s_ref[...] = jnp.zeros_like(s_ref)

    def chunk(j, carry):
        for d in range(2):
            cj = j if d == 0 else nch - 1 - j
            st = pl.multiple_of(cj * L, L)
            q_ref, k_ref, v_ref, gd_ref, o_ref = (
                (qf, kf, vf, gdf, of_ref) if d == 0 else (qb, kb, vb, gdb, ob_ref))
            gd = gd_ref[0, pl.ds(st, L), :].astype(bf16)
            la = _log_sigmoid(_dot(gd, gu_ref[d]) + gbias_ref[d]) * (1.0 / GLA_TAU)
            la_hi = la.astype(bf16)
            la_lo = (la - la_hi.astype(f32)).astype(bf16)
            cm = cm_ref[d]
            e_all = _dot(cm, la_hi) + _dot(cm, la_lo)
            for h in range(H):
                u = d * H + h
                ks = slice(h * dk, (h + 1) * dk)
                b = e_all[0:L, ks]
                bl = b[L - 1:L, :] if d == 0 else b[0:1, :]
                q = q_ref[0, pl.ds(st, L), ks]
                k = k_ref[0, pl.ds(st, L), ks]
                v = v_ref[0, pl.ds(st, L), h * dv:(h + 1) * dv]
                qs = q.astype(f32) * scale
                kf32 = k.astype(f32)
                st_t = s_ref[u]

                a = jnp.where(pm_ref[d, GLA_LEVELS] > 0, _dot_nt(qs.astype(bf16), k), 0.0)
                for lv in range(GLA_LEVELS):
                    x = jnp.exp(e_all[(1 + lv) * L:(2 + lv) * L, ks])
                    a_lv = _dot_nt((qs * x).astype(bf16), (kf32 * x).astype(bf16))
                    a = jnp.where(pm_ref[d, lv] > 0, a_lv, a)
                o = _dot(a.astype(bf16), v) + _dot_nt((qs * jnp.exp(b)).astype(bf16), st_t.astype(bf16))
                o_ref[0, pl.ds(st, L), h * dv:(h + 1) * dv] = o
                kt = (kf32 * jnp.exp(bl - b)).astype(bf16)
                s_ref[u] = st_t * jnp.exp(bl) + _dot_tn(v, kt)
        return carry

    lax.fori_loop(0, nch, chunk, 0)


def _gla(qk, v, gd, gate_up, gate_b, tb=512):
    B, S, _ = qk.shape
    tb = min(tb, S)
    assert S % tb == 0 and tb % CHUNK == 0
    nb = S // tb
    nch = tb // CHUNK
    R = GLA_GATE_RANK
    gu = jnp.zeros((2, LANE, GLA_KW), f32)
    gu = gu.at[0, :R].set(gate_up[0]).at[1, R:2 * R].set(gate_up[1]).astype(bf16)
    gbias = gate_b.astype(f32).reshape(2, 1, GLA_KW)
    cm_np, pm_np = _gla_tables()
    cm = jnp.asarray(cm_np, bf16)
    pm = jnp.asarray(pm_np, f32)

    def fwd(c):
        return lambda b, i: (b, i, c)

    def bwd(c):
        return lambda b, i: (b, nb - 1 - i, c)

    const3 = lambda b, i: (0, 0, 0)
    in_specs = []
    for mk in (fwd, bwd):
        in_specs += [pl.BlockSpec((1, tb, GLA_KW), mk(0)), pl.BlockSpec((1, tb, GLA_KW), mk(1)),
                     pl.BlockSpec((1, tb, GLA_VW), mk(0)), pl.BlockSpec((1, tb, LANE), mk(0))]
    in_specs += [pl.BlockSpec((2, LANE, GLA_KW), const3), pl.BlockSpec((2, 1, GLA_KW), const3),
                 pl.BlockSpec(cm.shape, const3), pl.BlockSpec(pm.shape, lambda b, i: (0, 0, 0, 0))]
    return pl.pallas_call(
        functools.partial(_gla_body, nch=nch),
        grid=(B, nb),
        in_specs=in_specs,
        out_specs=[pl.BlockSpec((1, tb, GLA_VW), fwd(0)), pl.BlockSpec((1, tb, GLA_VW), bwd(0))],
        out_shape=[jax.ShapeDtypeStruct((B, S, GLA_VW), f32)] * 2,
        scratch_shapes=[pltpu.VMEM((2 * GLA_HEADS, GLA_DV, GLA_DK), f32)],
        compiler_params=_cparams(("parallel", "arbitrary")),
        name="gla",
    )(qk, qk, v, gd, qk, qk, v, gd, gu, gbias, cm, pm)


def _odd_out_body(x_ref, of_ref, ob_ref, r_ref, g_ref, w_ref, gp_ref, out_ref):
    o = of_ref[...] + ob_ref[...]
    hn = jnp.concatenate(
        [_rms(o[:, h * GLA_DV:(h + 1) * GLA_DV]) for h in range(GLA_HEADS)], axis=-1)
    r = r_ref[...].astype(f32)
    y = (hn * g_ref[...]) * (r * _sigmoid(r))
    m = _dot(y.astype(bf16), w_ref[...])
    out_ref[...] = x_ref[...] + _rms(m) * gp_ref[...]


def _odd_out(x2d, of2d, ob2d, r2d, norm_g, w_out, g_post, tm=512):
    T, Dm = x2d.shape
    row = lambda i: (i, 0)
    const = lambda i: (0, 0)
    return pl.pallas_call(
        _odd_out_body,
        grid=(T // tm,),
        in_specs=[pl.BlockSpec((tm, Dm), row), pl.BlockSpec((tm, GLA_VW), row),
                  pl.BlockSpec((tm, GLA_VW), row), pl.BlockSpec((tm, GLA_VW), row),
                  pl.BlockSpec((1, GLA_VW), const), pl.BlockSpec((GLA_VW, Dm), const),
                  pl.BlockSpec((1, Dm), const)],
        out_specs=pl.BlockSpec((tm, Dm), row),
        out_shape=jax.ShapeDtypeStruct((T, Dm), f32),
        compiler_params=_cparams(("parallel",)),
        name="odd_out",
    )(x2d, of2d, ob2d, r2d, norm_g.reshape(1, GLA_VW), w_out, g_post.reshape(1, Dm))


FFN_HALO = 16
FFN_COLS = 256


def _ffn_tail_body(x_ref, u_ref, up_ref, un_ref, cw_ref, cb_ref, wd_ref, g_ref, p_ref,
                   wg_ref, wp_ref, out_ref, *, tm):
    i = pl.program_id(1)
    last = pl.num_programs(1) - 1
    row = lax.broadcasted_iota(jnp.int32, (tm, 1), 0)
    is_first_row = row == 0
    is_last_row = row == tm - 1
    keep_prev = (i > 0).astype(f32)
    keep_next = (i < last).astype(f32)

    def conv(c0):
        cs = slice(c0, c0 + FFN_COLS)
        u = u_ref[0, :, cs].astype(f32)
        prev = up_ref[0, FFN_HALO - 1:FFN_HALO, cs].astype(f32) * keep_prev
        nxt = un_ref[0, 0:1, cs].astype(f32) * keep_next
        above = jnp.where(is_first_row, prev, pltpu.roll(u, 1, axis=0))
        below = jnp.where(is_last_row, nxt, pltpu.roll(u, tm - 1, axis=0))
        return above * cw_ref[0:1, cs] + u * cw_ref[1:2, cs] + below * cw_ref[2:3, cs] + cb_ref[:, cs]

    d = jnp.zeros((tm, D_MODEL), f32)
    for c0 in range(0, D_FF, FFN_COLS):
        act = (_gelu_tanh(conv(D_FF + c0)) * conv(c0)).astype(bf16)
        d = d + _dot(act, wd_ref[c0:c0 + FFN_COLS, :])
    x2 = x_ref[0] + _rms(d) * g_ref[...]
    gate = _sigmoid(_dot(x2.astype(bf16), wg_ref[...]))
    out_ref[0] = x2 + _dot(p_ref[0].astype(bf16), wp_ref[...]) * gate


def _ffn_tail(x3, u3, conv_w, conv_b, w_down, g_post, p3, w_gate, w_proj, tm=512):
    B, S, Dm = x3.shape
    NF = u3.shape[-1]
    tm = min(tm, S)
    assert S % tm == 0 and tm % FFN_HALO == 0 and D_FF % FFN_COLS == 0
    hb = tm // FFN_HALO
    nhalo = S // FFN_HALO
    blk = lambda b, i: (b, i, 0)
    const = lambda b, i: (0, 0)
    return pl.pallas_call(
        functools.partial(_ffn_tail_body, tm=tm),
        grid=(B, S // tm),
        in_specs=[pl.BlockSpec((1, tm, Dm), blk),
                  pl.BlockSpec((1, tm, NF), blk),
                  pl.BlockSpec((1, FFN_HALO, NF), lambda b, i: (b, jnp.maximum(i * hb - 1, 0), 0)),
                  pl.BlockSpec((1, FFN_HALO, NF), lambda b, i: (b, jnp.minimum((i + 1) * hb, nhalo - 1), 0)),
                  pl.BlockSpec((3, NF), const), pl.BlockSpec((1, NF), const),
                  pl.BlockSpec((D_FF, Dm), const), pl.BlockSpec((1, Dm), const),
                  pl.BlockSpec((1, tm, p3.shape[-1]), blk),
                  pl.BlockSpec((Dm, Dm), const), pl.BlockSpec((p3.shape[-1], Dm), const)],
        out_specs=pl.BlockSpec((1, tm, Dm), blk),
        out_shape=jax.ShapeDtypeStruct((B, S, Dm), f32),
        compiler_params=_cparams(("parallel", "parallel")),
        name="ffn_tail",
    )(x3, u3, u3, u3, conv_w, conv_b.reshape(1, NF), w_down, g_post.reshape(1, Dm), p3, w_gate, w_proj)


def _pad_cols(w, n):
    return jnp.pad(w, ((0, 0), (0, n - w.shape[1])))


def _trunk(x, p, norm_mix_pre, norm_mix_post, norm_ffn_pre, norm_ffn_post,
           even_w_in, even_na_rpb, even_ml_gate_b, even_ml_norm_g, even_w_out,
           odd_w_in, odd_gate_up, odd_gate_b, odd_norm_g, odd_w_out,
           ffn_w_up, ffn_conv_w, ffn_conv_b, ffn_w_down, ple_w_proj, ple_w_gate):
    B, S, Dm = x.shape
    T = B * S
    depth = norm_mix_pre.shape[0]
    for i in range(depth):
        j = i // 2
        x2d = x.reshape(T, Dm)
        if i % 2 == 0:
            widths = (3 * NA_WIDTH, 4 * ML_WIDTH, LANE)
            w_in = _pad_cols(even_w_in[j], sum(widths)).astype(bf16)
            a3, m4, g = _norm_matmul(x2d, norm_mix_pre[i], w_in, widths, (bf16, bf16, f32))
            y_na = _neighborhood_attention(a3.reshape(B, S, -1), even_na_rpb[j])
            hf, hb = _mlstm(m4.reshape(B, S, -1), g.reshape(B, S, LANE), even_ml_gate_b[j])
            x2d = _even_out(x2d, y_na.reshape(T, -1), hf.reshape(T, -1), hb.reshape(T, -1), m4,
                            even_ml_norm_g[j], even_w_out[j].astype(bf16), norm_mix_post[i])
        else:
            widths = (2 * GLA_KW, GLA_VW, GLA_VW, LANE)
            w_in = _pad_cols(odd_w_in[j], sum(widths)).astype(bf16)
            qk, v, r, gd = _norm_matmul(x2d, norm_mix_pre[i], w_in, widths, (bf16, bf16, bf16, f32))
            of, ob = _gla(qk.reshape(B, S, -1), v.reshape(B, S, -1), gd.reshape(B, S, LANE),
                          odd_gate_up[j], odd_gate_b[j])
            x2d = _odd_out(x2d, of.reshape(T, -1), ob.reshape(T, -1), r, odd_norm_g[j],
                           odd_w_out[j].astype(bf16), norm_mix_post[i])
        (u,) = _norm_matmul(x2d, norm_ffn_pre[i], ffn_w_up[i].astype(bf16), (2 * D_FF,), (bf16,))
        x = _ffn_tail(x2d.reshape(B, S, Dm), u.reshape(B, S, -1), ffn_conv_w[i], ffn_conv_b[i],
                      ffn_w_down[i].astype(bf16), norm_ffn_post[i], p[i],
                      ple_w_gate[i].astype(bf16), ple_w_proj[i].astype(bf16))
    return x


def kernel(x_prompt, x_sample, p_prompt, p_sample, norm_mix_pre, norm_mix_post, norm_ffn_pre, norm_ffn_post, even_w_in, even_na_rpb, even_ml_gate_b, even_ml_norm_g, even_w_out, odd_w_in, odd_gate_up, odd_gate_b, odd_norm_g, odd_w_out, ffn_w_up, ffn_conv_w, ffn_conv_b, ffn_w_down, ple_w_proj, ple_w_gate):
    params = (norm_mix_pre, norm_mix_post, norm_ffn_pre, norm_ffn_post,
              even_w_in, even_na_rpb, even_ml_gate_b, even_ml_norm_g, even_w_out,
              odd_w_in, odd_gate_up, odd_gate_b, odd_norm_g, odd_w_out,
              ffn_w_up, ffn_conv_w, ffn_conv_b, ffn_w_down, ple_w_proj, ple_w_gate)
    return (_trunk(x_prompt, p_prompt, *params), _trunk(x_sample, p_sample, *params))
```

```python
import functools

import jax
import jax.numpy as jnp
import numpy as np
from jax import lax
from jax.experimental import pallas as pl
from jax.experimental.pallas import tpu as pltpu

f32 = jnp.float32
bf16 = jnp.bfloat16

D_MODEL = 1024
EPS = 1e-6
GRID_W = 64
NA_HEADS = 8
NA_HEAD_DIM = 64
NA_WIN_ROWS = 8
NA_WIN_COLS = 16
NA_GROUP = 4
NA_WIDTH = NA_HEADS * NA_HEAD_DIM
ML_HEADS = 4
ML_HEAD_DIM = 128
ML_WIDTH = ML_HEADS * ML_HEAD_DIM
GLA_HEADS = 4
GLA_DK = 128
GLA_DV = 256
GLA_KW = GLA_HEADS * GLA_DK
GLA_VW = GLA_HEADS * GLA_DV
GLA_GATE_RANK = 16
GLA_TAU = 16.0
CHUNK = 64
D_FF = 2816
LANE = 128
NEG = -1e30
LOG2E = 1.4426950408889634

VMEM_LIMIT = 56 * 1024 * 1024


def _cparams(sem):
    return pltpu.CompilerParams(dimension_semantics=sem, vmem_limit_bytes=VMEM_LIMIT)


def _dot(a, b):
    return jnp.dot(a, b, preferred_element_type=f32)


def _dot_nt(a, b):
    return lax.dot_general(a, b, (((1,), (1,)), ((), ())), preferred_element_type=f32)


def _dot_tn(a, b):
    return lax.dot_general(a, b, (((0,), (0,)), ((), ())), preferred_element_type=f32)


def _dot_f32(a, b):
    return jnp.dot(a, b, precision=lax.Precision.HIGHEST, preferred_element_type=f32)


def _rms(x):
    return x * lax.rsqrt(jnp.mean(x * x, axis=-1, keepdims=True) + EPS)


def _sigmoid(x):
    return 1.0 / (1.0 + jnp.exp(-x))


def _log_sigmoid(x):
    return jnp.minimum(x, 0.0) - jnp.log(1.0 + jnp.exp(-jnp.abs(x)))


def _norm_matmul_body(x_ref, g_ref, w_ref, *o_refs, widths):
    hn = (_rms(x_ref[...]) * g_ref[...]).astype(bf16)
    off = 0
    for o_ref, n in zip(o_refs, widths):
        for c0 in range(0, n, 512):
            cw = min(512, n - c0)
            o_ref[:, c0:c0 + cw] = _dot(hn, w_ref[:, off + c0:off + c0 + cw]).astype(o_ref.dtype)
        off += n


def _norm_matmul(x2d, g, w, widths, dtypes, tm=512):
    T, Dm = x2d.shape
    N = w.shape[1]
    assert sum(widths) == N and T % tm == 0
    return pl.pallas_call(
        functools.partial(_norm_matmul_body, widths=tuple(widths)),
        grid=(T // tm,),
        in_specs=[pl.BlockSpec((tm, Dm), lambda i: (i, 0)),
                  pl.BlockSpec((1, Dm), lambda i: (0, 0)),
                  pl.BlockSpec((Dm, N), lambda i: (0, 0))],
        out_specs=[pl.BlockSpec((tm, n), lambda i: (i, 0)) for n in widths],
        out_shape=[jax.ShapeDtypeStruct((T, n), dt) for n, dt in zip(widths, dtypes)],
        compiler_params=_cparams(("parallel",)),
        name="norm_matmul",
    )(x2d, g.reshape(1, Dm), w)


def _na_body(q_ref, k_ref, v_ref, bias_ref, o_ref, *, R, rows):
    i = pl.program_id(2)
    lane = lax.broadcasted_iota(jnp.int32, (GRID_W, LANE), 1)
    first = lane < NA_HEAD_DIM
    nkeys = NA_WIN_ROWS * GRID_W

    def group_body(jg, carry):
        qoffs, kstarts, cases = [], [], []
        for t in range(NA_GROUP):
            j = jg * NA_GROUP + t
            r = i * R + j
            rs = jnp.clip(r - NA_WIN_ROWS // 2, 0, rows - NA_WIN_ROWS)
            cases.append(r - rs)
            qoffs.append(pl.multiple_of(j * GRID_W, GRID_W))
            kstarts.append(pl.multiple_of(rs * GRID_W, GRID_W))
        s_all = []
        for t in range(NA_GROUP):
            q = q_ref[0, pl.ds(qoffs[t], GRID_W), :]
            zero = jnp.zeros_like(q)
            q2 = jnp.concatenate([jnp.where(first, q, zero), jnp.where(first, zero, q)], axis=0)
            kw = k_ref[0, pl.ds(kstarts[t], nkeys), :]
            s = _dot_nt(q2, kw) * (NA_HEAD_DIM ** -0.5 * LOG2E)
            s_all.append(s + jnp.concatenate([bias_ref[cases[t], 0], bias_ref[cases[t], 1]], axis=0))
        p_all, l_all = [], []
        for s in s_all:
            p = jnp.exp2(s - jnp.max(s, axis=-1, keepdims=True))
            l_all.append(jnp.sum(p, axis=-1, keepdims=True))
            p_all.append(p.astype(bf16))
        for t in range(NA_GROUP):
            vw = v_ref[0, pl.ds(kstarts[t], nkeys), :]
            o = _dot(p_all[t], vw) / l_all[t]
            o_ref[0, pl.ds(qoffs[t], GRID_W), :] = jnp.where(
                first, o[:GRID_W], o[GRID_W:]).astype(o_ref.dtype)
        return carry

    lax.fori_loop(0, R // NA_GROUP, group_body, 0)


def _na_bias_table(rpb):
    cols = np.arange(GRID_W)
    cs = np.clip(cols - NA_WIN_COLS // 2, 0, GRID_W - NA_WIN_COLS)
    kc = np.arange(GRID_W)
    valid = (kc[None, :] >= cs[:, None]) & (kc[None, :] < cs[:, None] + NA_WIN_COLS)
    dc = np.clip(kc[None, :] - cols[:, None] + (NA_WIN_COLS - 1), 0, 2 * NA_WIN_COLS - 2)
    off = np.arange(NA_WIN_ROWS)
    dr = np.arange(NA_WIN_ROWS)[None, :] - off[:, None] + (NA_WIN_ROWS - 1)
    t = rpb[:, dr][:, :, :, dc]
    t = jnp.where(valid[None, None, None], t.astype(f32) * LOG2E, NEG)
    t = jnp.transpose(t, (1, 0, 3, 2, 4))
    return t.reshape(NA_WIN_ROWS, NA_HEADS, GRID_W, NA_WIN_ROWS * GRID_W)


def _neighborhood_attention(a3, rpb, R=8):
    B, S, _ = a3.shape
    rows = S // GRID_W
    assert rows >= NA_WIN_ROWS and rows % R == 0 and R % NA_GROUP == 0
    bias = _na_bias_table(rpb)
    npair = NA_WIDTH // LANE
    return pl.pallas_call(
        functools.partial(_na_body, R=R, rows=rows),
        grid=(B, npair, rows // R),
        in_specs=[pl.BlockSpec((1, R * GRID_W, LANE), lambda b, p, i: (b, i, p)),
                  pl.BlockSpec((1, S, LANE), lambda b, p, i: (b, 0, npair + p)),
                  pl.BlockSpec((1, S, LANE), lambda b, p, i: (b, 0, 2 * npair + p)),
                  pl.BlockSpec((NA_WIN_ROWS, 2, GRID_W, NA_WIN_ROWS * GRID_W), lambda b, p, i: (0, p, 0, 0))],
        out_specs=pl.BlockSpec((1, R * GRID_W, LANE), lambda b, p, i: (b, i, p)),
        out_shape=jax.ShapeDtypeStruct((B, S, NA_WIDTH), bf16),
        compiler_params=_cparams(("parallel", "parallel", "arbitrary")),
        name="neighborhood_attention",
    )(a3, a3, a3, bias)


def _mlstm_tables():
    L = CHUNK
    t = np.arange(L)
    tril = (t[None, :] <= t[:, None]).astype(np.float32)
    masks = np.stack([tril, tril.T, np.eye(L, dtype=np.float32)])
    rhs = np.zeros((2 * L, 2 * LANE), np.float32)
    rhs[:L, :LANE] = 1.0
    rhs[:L, LANE:] = -1.0
    rhs[L:, LANE:] = 1.0
    return masks, rhs


def _mlstm_body(qf, kf, vf, qb, kb, vb, gtf, gtb, bcol, msk_ref, rhs_ref, hf_ref, hb_ref,
                c_ref, m_ref, *, nch):
    L = CHUNK
    H = ML_HEADS
    dh = ML_HEAD_DIM
    scale = dh ** -0.5
    units = [(d, h) for d in range(2) for h in range(H)]

    @pl.when(pl.program_id(1) == 0)
    def _():
        c_ref[...] = jnp.zeros_like(c_ref)
        m_ref[...] = jnp.zeros_like(m_ref)

    sub = lax.broadcasted_iota(jnp.int32, (4 * H, 1), 0)
    f_row = (sub // H) % 2 == 1

    def chunk(j, carry):
        cjs = (j, nch - 1 - j)
        sts = [pl.multiple_of(cj * L, L) for cj in cjs]
        tril, triu, eye = msk_ref[0], msk_ref[1], msk_ref[2]
        rhs = rhs_ref[...]
        gts, brows = [], []
        for d, gt_ref in enumerate((gtf, gtb)):
            gt = gt_ref[0, cjs[d]] + bcol[...]
            gt = jnp.where(f_row, _log_sigmoid(gt), gt)
            gts.append(gt)
            brows.append(_dot_f32(gt, triu if d == 0 else tril))

        bcols, wcols = [], []
        for d, h in units:
            li_row = gts[d][d * 2 * H + h:d * 2 * H + h + 1, :]
            lf_row = gts[d][d * 2 * H + H + h:d * 2 * H + H + h + 1, :]
            lhs = jnp.concatenate([(tril if d == 0 else triu) * lf_row, eye * li_row], axis=1)
            hi = lhs.astype(bf16)
            lo = (lhs - hi.astype(f32)).astype(bf16)
            p = _dot(hi, rhs) + _dot(lo, rhs)
            bcols.append(p[:, :LANE])
            wcols.append(p[:, LANE:])

        qs, ks, vs, qks = [], [], [], []
        for d, h in units:
            q_ref, k_ref, v_ref = (qf, kf, vf) if d == 0 else (qb, kb, vb)
            hs = slice(h * dh, (h + 1) * dh)
            qs.append(q_ref[0, pl.ds(sts[d], L), hs])
            ks.append(k_ref[0, pl.ds(sts[d], L), hs])
            vs.append(v_ref[0, pl.ds(sts[d], L), hs])
            qks.append(_dot_nt(qs[-1], ks[-1]))

        scs, aiss, mts = [], [], []
        for u, (d, h) in enumerate(units):
            li_row = gts[d][d * 2 * H + h:d * 2 * H + h + 1, :]
            b_row = brows[d][d * 2 * H + H + h:d * 2 * H + H + h + 1, :]
            m_prev = m_ref[u]
            dm = jnp.where((tril if d == 0 else triu) > 0, bcols[u][:, :L] - b_row + li_row, NEG)
            inter = bcols[u] + m_prev
            mt = jnp.maximum(inter, jnp.max(dm, axis=-1, keepdims=True))
            scs.append(qks[u] * scale * jnp.exp(dm - mt[:, :L]))
            aiss.append(jnp.exp(inter - mt) * scale)
            mts.append(mt)

        nvs, qcs = [], []
        for u in range(len(units)):
            nvs.append(_dot(scs[u].astype(bf16), vs[u]))
            qcs.append(_dot(qs[u], c_ref[u].astype(bf16)))

        for u, (d, h) in enumerate(units):
            h_ref = hf_ref if d == 0 else hb_ref
            num = nvs[u] + aiss[u] * qcs[u][:, :dh]
            den = jnp.sum(scs[u], axis=-1, keepdims=True) + aiss[u] * qcs[u][:, dh:]
            h_ref[0, pl.ds(sts[d], L), h * dh:(h + 1) * dh] = num / jnp.maximum(jnp.abs(den), jnp.exp(-mts[u]))

        for u, (d, h) in enumerate(units):
            m_prev = m_ref[u]
            bl = bcols[u][L - 1:L, :] if d == 0 else bcols[u][0:1, :]
            ds = wcols[u] + bl
            m_new = jnp.maximum(bl + m_prev, jnp.max(ds, axis=0, keepdims=True))
            decay = jnp.exp(bl + m_prev - m_new)
            ws = jnp.exp(ds - m_new)
            wv = jnp.concatenate([ws * vs[u].astype(f32), ws], axis=1).astype(bf16)
            c_ref[u] = jnp.concatenate([decay, decay], axis=1) * c_ref[u] + _dot_tn(ks[u], wv)
            m_ref[u] = m_new
        return carry

    lax.fori_loop(0, nch, chunk, 0)


def _mlstm(m4, g, gate_b, tb=512):
    B, S, _ = m4.shape
    tb = min(tb, S)
    assert S % tb == 0 and tb % CHUNK == 0
    nb = S // tb
    nch = tb // CHUNK
    ng = 4 * ML_HEADS
    gt = jnp.swapaxes(g[..., :ng].reshape(B, S // CHUNK, CHUNK, ng), 2, 3)
    bcol = gate_b.astype(f32).reshape(ng, 1)
    masks_np, rhs_np = _mlstm_tables()
    masks = jnp.asarray(masks_np, f32)
    rhs = jnp.asarray(rhs_np, bf16)
    W = ML_WIDTH

    def fwd(c):
        return lambda b, i: (b, i, c)

    def bwd(c):
        return lambda b, i: (b, nb - 1 - i, c)

    in_specs = (
        [pl.BlockSpec((1, tb, W), fwd(c)) for c in range(3)]
        + [pl.BlockSpec((1, tb, W), bwd(c)) for c in range(3)]
        + [pl.BlockSpec((1, nch, ng, CHUNK), lambda b, i: (b, i, 0, 0)),
           pl.BlockSpec((1, nch, ng, CHUNK), lambda b, i: (b, nb - 1 - i, 0, 0)),
           pl.BlockSpec((ng, 1), lambda b, i: (0, 0)),
           pl.BlockSpec(masks.shape, lambda b, i: (0, 0, 0)),
           pl.BlockSpec(rhs.shape, lambda b, i: (0, 0))])
    return pl.pallas_call(
        functools.partial(_mlstm_body, nch=nch),
        grid=(B, nb),
        in_specs=in_specs,
        out_specs=[pl.BlockSpec((1, tb, W), fwd(0)), pl.BlockSpec((1, tb, W), bwd(0))],
        out_shape=[jax.ShapeDtypeStruct((B, S, W), f32)] * 2,
        scratch_shapes=[pltpu.VMEM((2 * ML_HEADS, ML_HEAD_DIM, 2 * ML_HEAD_DIM), f32),
                        pltpu.VMEM((2 * ML_HEADS, 1, LANE), f32)],
        compiler_params=_cparams(("parallel", "arbitrary")),
        name="mlstm",
    )(m4, m4, m4, m4, m4, m4, gt, gt, bcol, masks, rhs)


def _even_out_body(x_ref, na_ref, hf_ref, hb_ref, o_ref, g_ref, w_ref, gp_ref, out_ref):
    hm = hf_ref[...] + hb_ref[...]
    hn = jnp.concatenate(
        [_rms(hm[:, h * ML_HEAD_DIM:(h + 1) * ML_HEAD_DIM]) for h in range(ML_HEADS)], axis=-1)
    y_ml = _sigmoid(o_ref[...].astype(f32)) * (hn * g_ref[...])
    m = _dot(na_ref[...], w_ref[:NA_WIDTH, :]) + _dot(y_ml.astype(bf16), w_ref[NA_WIDTH:, :])
    out_ref[...] = x_ref[...] + _rms(m) * gp_ref[...]


def _even_out(x2d, na2d, hf2d, hb2d, m4_2d, ml_g, w_out, g_post, tm=512):
    T, Dm = x2d.shape
    W = ML_WIDTH
    row = lambda i: (i, 0)
    const = lambda i: (0, 0)
    return pl.pallas_call(
        _even_out_body,
        grid=(T // tm,),
        in_specs=[pl.BlockSpec((tm, Dm), row), pl.BlockSpec((tm, NA_WIDTH), row),
                  pl.BlockSpec((tm, W), row), pl.BlockSpec((tm, W), row),
                  pl.BlockSpec((tm, W), lambda i: (i, 3)),
                  pl.BlockSpec((1, W), const), pl.BlockSpec((NA_WIDTH + W, Dm), const),
                  pl.BlockSpec((1, Dm), const)],
        out_specs=pl.BlockSpec((tm, Dm), row),
        out_shape=jax.ShapeDtypeStruct((T, Dm), f32),
        compiler_params=_cparams(("parallel",)),
        name="even_out",
    )(x2d, na2d, hf2d, hb2d, m4_2d, ml_g.reshape(1, W), w_out, g_post.reshape(1, Dm))


GLA_BCAST_LEVELS = (32, 16, 8, 4)
T_SGN, T_COEF, T_PAIR = 0, 4, 8


def _gla_tables():
    L = CHUNK
    t = np.arange(L)
    tril = (t[None, :] <= t[:, None]).astype(np.float32)
    cum = np.stack([tril, tril.T])
    tbl = np.zeros((2, 12, L, LANE), np.float32)
    pm = np.zeros((2, 7, L, L), np.float32)
    for lv, g in enumerate((32, 16, 8, 4, 2, 1)):
        a = (t // (2 * g)) * 2 * g
        upper = (t - a) >= g
        same = a[:, None] == a[None, :]
        pm[0, lv] = same & upper[:, None] & (~upper)[None, :]
        pm[1, lv] = same & (~upper)[:, None] & upper[None, :]
        if g >= 4:
            tbl[0, T_SGN + lv] = np.where(upper, 1.0, -1.0)[:, None]
            tbl[1, T_SGN + lv] = np.where(upper, -1.0, 1.0)[:, None]
    pm[:, 6] = np.eye(L)
    r4, r2 = t % 4, t % 2
    tbl[0, T_COEF + 0] = np.isin(r4, (2, 3))[:, None]
    tbl[0, T_COEF + 1] = (r4 == 3)[:, None]
    tbl[0, T_COEF + 2] = (r4 == 0)[:, None]
    tbl[0, T_COEF + 3] = (r2 == 1)[:, None]
    tbl[1, T_COEF + 0] = np.isin(r4, (0, 1))[:, None]
    tbl[1, T_COEF + 1] = (r4 == 3)[:, None]
    tbl[1, T_COEF + 2] = (r4 == 0)[:, None]
    tbl[1, T_COEF + 3] = (r2 == 0)[:, None]
    for p in range(4):
        tbl[:, T_PAIR + p, :, :L] = pm[:, 2 * p]
        if p < 3:
            tbl[:, T_PAIR + p, :, L:] = pm[:, 2 * p + 1]
    return cum, tbl


def _gla_body(qf, kf, vf, gdf, qb, kb, vb, gdb, gu_ref, gbias_ref, cum_ref, tbl_ref,
              of_ref, ob_ref, s_ref, *, nch):
    L = CHUNK
    H = GLA_HEADS
    dk = GLA_DK
    dv = GLA_DV
    scale = dk ** -0.5
    units = [(d, h) for d in range(2) for h in range(H)]

    @pl.when(pl.program_id(1) == 0)
    def _():
        s_ref[...] = jnp.zeros_like(s_ref)

    def chunk(j, carry):
        cjs = (j, nch - 1 - j)
        sts = [pl.multiple_of(cj * L, L) for cj in cjs]
        zk = jnp.zeros((L, dk), bf16)

        las, bs = [], []
        for d, gd_ref in enumerate((gdf, gdb)):
            gd = gd_ref[0, pl.ds(sts[d], L), :].astype(bf16)
            la = _log_sigmoid(_dot(gd, gu_ref[d]) + gbias_ref[d]) * (LOG2E / GLA_TAU)
            hi = la.astype(bf16)
            lo = (la - hi.astype(f32)).astype(bf16)
            las.append(la)
            bs.append(_dot(cum_ref[d], hi) + _dot(cum_ref[d], lo))

        qss, kfs, vs, rs = [], [], [], []
        for d, h in units:
            q_ref, k_ref, v_ref = (qf, kf, vf) if d == 0 else (qb, kb, vb)
            ks = slice(h * dk, (h + 1) * dk)
            la = las[d][:, ks]
            b = bs[d][:, ks]
            k = k_ref[0, pl.ds(sts[d], L), ks]
            qs = q_ref[0, pl.ds(sts[d], L), ks].astype(f32) * scale
            kf32 = k.astype(f32)
            xs = []
            for lv, g in enumerate(GLA_BCAST_LEVELS):
                refs = [a + g - 1 + d for a in range(0, L, 2 * g)]
                bref = jnp.concatenate(
                    [jnp.broadcast_to(b[r:r + 1, :], (2 * g, dk)) for r in refs], axis=0)
                xs.append(jnp.exp2((b - bref) * tbl_ref[d, T_SGN + lv]))
            la_dn = pltpu.roll(la, 1, axis=0)
            la_up = pltpu.roll(la, L - 1, axis=0)
            xs.append(jnp.exp2(la * tbl_ref[d, T_COEF] + la_dn * tbl_ref[d, T_COEF + 1]
                               + la_up * tbl_ref[d, T_COEF + 2]))
            xs.append(jnp.exp2(la * tbl_ref[d, T_COEF + 3]))
            r_u = []
            for p in range(3):
                xa, xb = xs[2 * p], xs[2 * p + 1]
                lhs = jnp.concatenate([(qs * xa).astype(bf16), (qs * xb).astype(bf16)], axis=1)
                rhs = jnp.concatenate(
                    [jnp.concatenate([(kf32 * xa).astype(bf16), zk], axis=1),
                     jnp.concatenate([zk, (kf32 * xb).astype(bf16)], axis=1)], axis=0)
                r_u.append(_dot_nt(lhs, rhs))
            r_u.append(_dot_nt(qs.astype(bf16), jnp.concatenate([k, zk], axis=0)))
            rs.append(r_u)
            qss.append(qs)
            kfs.append(kf32)
            vs.append(v_ref[0, pl.ds(sts[d], L), h * dv:(h + 1) * dv])

        for u, (d, h) in enumerate(units):
            o_ref = of_ref if d == 0 else ob_ref
            b = bs[d][:, h * dk:(h + 1) * dk]
            a2 = jnp.where(tbl_ref[d, T_PAIR + 3] > 0, rs[u][3], 0.0)
            for p in range(2, -1, -1):
                a2 = jnp.where(tbl_ref[d, T_PAIR + p] > 0, rs[u][p], a2)
            v2 = jnp.concatenate([vs[u], vs[u]], axis=0)
            o = _dot(a2.astype(bf16), v2) + _dot_nt((qss[u] * jnp.exp2(b)).astype(bf16),
                                                    s_ref[u].astype(bf16))
            o_ref[0, pl.ds(sts[d], L), h * dv:(h + 1) * dv] = o

        for u, (d, h) in enumerate(units):
            b = bs[d][:, h * dk:(h + 1) * dk]
            bl = b[L - 1:L, :] if d == 0 else b[0:1, :]
            kt = (kfs[u] * jnp.exp2(bl - b)).astype(bf16)
            s_ref[u] = s_ref[u] * jnp.exp2(bl) + _dot_tn(vs[u], kt)
        return carry

    lax.fori_loop(0, nch, chunk, 0)


def _gla(qk, v, gd, gate_up, gate_b, tb=512):
    B, S, _ = qk.shape
    tb = min(tb, S)
    assert S % tb == 0 and tb % CHUNK == 0
    nb = S // tb
    nch = tb // CHUNK
    R = GLA_GATE_RANK
    gu = jnp.zeros((2, LANE, GLA_KW), f32)
    gu = gu.at[0, :R].set(gate_up[0]).at[1, R:2 * R].set(gate_up[1]).astype(bf16)
    gbias = gate_b.astype(f32).reshape(2, 1, GLA_KW)
    cum_np, tbl_np = _gla_tables()
    cum = jnp.asarray(cum_np, bf16)
    tbl = jnp.asarray(tbl_np, f32)

    def fwd(c):
        return lambda b, i: (b, i, c)

    def bwd(c):
        return lambda b, i: (b, nb - 1 - i, c)

    const3 = lambda b, i: (0, 0, 0)
    in_specs = []
    for mk in (fwd, bwd):
        in_specs += [pl.BlockSpec((1, tb, GLA_KW), mk(0)), pl.BlockSpec((1, tb, GLA_KW), mk(1)),
                     pl.BlockSpec((1, tb, GLA_VW), mk(0)), pl.BlockSpec((1, tb, LANE), mk(0))]
    in_specs += [pl.BlockSpec((2, LANE, GLA_KW), const3), pl.BlockSpec((2, 1, GLA_KW), const3),
                 pl.BlockSpec(cum.shape, const3), pl.BlockSpec(tbl.shape, lambda b, i: (0, 0, 0, 0))]
    return pl.pallas_call(
        functools.partial(_gla_body, nch=nch),
        grid=(B, nb),
        in_specs=in_specs,
        out_specs=[pl.BlockSpec((1, tb, GLA_VW), fwd(0)), pl.BlockSpec((1, tb, GLA_VW), bwd(0))],
        out_shape=[jax.ShapeDtypeStruct((B, S, GLA_VW), f32)] * 2,
        scratch_shapes=[pltpu.VMEM((2 * GLA_HEADS, GLA_DV, GLA_DK), f32)],
        compiler_params=_cparams(("parallel", "arbitrary")),
        name="gla",
    )(qk, qk, v, gd, qk, qk, v, gd, gu, gbias, cum, tbl)


def _odd_out_body(x_ref, of_ref, ob_ref, r_ref, g_ref, w_ref, gp_ref, out_ref):
    o = of_ref[...] + ob_ref[...]
    hn = jnp.concatenate(
        [_rms(o[:, h * GLA_DV:(h + 1) * GLA_DV]) for h in range(GLA_HEADS)], axis=-1)
    r = r_ref[...].astype(f32)
    y = (hn * g_ref[...]) * (r * _sigmoid(r))
    m = _dot(y.astype(bf16), w_ref[...])
    out_ref[...] = x_ref[...] + _rms(m) * gp_ref[...]


def _odd_out(x2d, of2d, ob2d, r2d, norm_g, w_out, g_post, tm=512):
    T, Dm = x2d.shape
    row = lambda i: (i, 0)
    const = lambda i: (0, 0)
    return pl.pallas_call(
        _odd_out_body,
        grid=(T // tm,),
        in_specs=[pl.BlockSpec((tm, Dm), row), pl.BlockSpec((tm, GLA_VW), row),
                  pl.BlockSpec((tm, GLA_VW), row), pl.BlockSpec((tm, GLA_VW), row),
                  pl.BlockSpec((1, GLA_VW), const), pl.BlockSpec((GLA_VW, Dm), const),
                  pl.BlockSpec((1, Dm), const)],
        out_specs=pl.BlockSpec((tm, Dm), row),
        out_shape=jax.ShapeDtypeStruct((T, Dm), f32),
        compiler_params=_cparams(("parallel",)),
        name="odd_out",
    )(x2d, of2d, ob2d, r2d, norm_g.reshape(1, GLA_VW), w_out, g_post.reshape(1, Dm))


FFN_HALO = 8
FFN_COLS = 256
GELU_K2 = 2.0 * 0.7978845608028654
GELU_C = 0.044715


def _ffn_body(x_ref, xn_ref, xp_ref, gpre_ref, wu_ref, cw_ref, cb_ref, wd_ref, gpost_ref, p_ref,
              wg_ref, wp_ref, out_ref, lhs_ref, ue_ref, acc_ref, *, tm):
    i = pl.program_id(1)
    keep_prev = (i > 0).astype(f32)
    keep_next = (i < pl.num_programs(1) - 1).astype(f32)
    gpre = gpre_ref[...]
    x = x_ref[0]
    halo = jnp.concatenate([_rms(xn_ref[0]) * gpre * keep_next, _rms(xp_ref[0]) * gpre * keep_prev], axis=0)
    lhs_ref[...] = jnp.concatenate([(_rms(x) * gpre).astype(bf16), halo.astype(bf16)], axis=0)
    acc_ref[...] = jnp.zeros_like(acc_ref)
    te = tm + 2 * FFN_HALO
    nchunks = D_FF // FFN_COLS

    def up(n, slot):
        lhs = lhs_ref[...]
        ue_ref[slot, 0] = _dot(lhs, wu_ref[nchunks + n])
        ue_ref[slot, 1] = _dot(lhs, wu_ref[n])

    def conv(n, slot, half):
        ue = ue_ref[slot, half]
        w = cw_ref[half * nchunks + n]
        above = pltpu.roll(ue, 1, axis=0)[:tm]
        below = pltpu.roll(ue, te - 1, axis=0)[:tm]
        return above * w[0:1] + ue[:tm] * w[1:2] + below * w[2:3] + cb_ref[half * nchunks + n]

    def down(n, slot):
        g = conv(n, slot, 0)
        e = jnp.exp2(g * (g * g * (-GELU_K2 * GELU_C * LOG2E) + (-GELU_K2 * LOG2E)))
        act = (g / (1.0 + e) * conv(n, slot, 1)).astype(bf16)
        acc_ref[...] += _dot(act, wd_ref[n])

    up(0, 0)

    def pair(jj, carry):
        n = 2 * jj
        up(n + 1, 1)
        down(n, 0)
        up(n + 2, 0)
        down(n + 1, 1)
        return carry

    lax.fori_loop(0, (nchunks - 1) // 2, pair, 0)
    down(nchunks - 1, 0)
    x2 = x + _rms(acc_ref[...]) * gpost_ref[...]
    gate = _sigmoid(_dot(x2.astype(bf16), wg_ref[...]))
    out_ref[0] = x2 + _dot(p_ref[0].astype(bf16), wp_ref[...]) * gate


def _resident(shape):
    nd = len(shape)
    return pl.BlockSpec(shape, lambda b, i: (0,) * nd, pipeline_mode=pl.Buffered(1))


def _ffn(x3, g_pre, w_up, conv_w, conv_b, w_down, g_post, p3, w_gate, w_proj, tm=512):
    B, S, Dm = x3.shape
    NF = w_up.shape[1]
    PD = p3.shape[-1]
    tm = min(tm, S)
    nchunks = D_FF // FFN_COLS
    assert S % tm == 0 and tm % FFN_HALO == 0 and D_FF % FFN_COLS == 0 and NF == 2 * D_FF and nchunks % 2 == 1
    wu3 = jnp.transpose(w_up.reshape(Dm, 2 * nchunks, FFN_COLS), (1, 0, 2))
    cw3 = jnp.transpose(conv_w.reshape(3, 2 * nchunks, FFN_COLS), (1, 0, 2))
    cw3 = jnp.concatenate([cw3[nchunks:], cw3[:nchunks]], axis=0)
    cb3 = conv_b.reshape(2 * nchunks, 1, FFN_COLS)
    cb3 = jnp.concatenate([cb3[nchunks:], cb3[:nchunks]], axis=0)
    wd3 = w_down.reshape(nchunks, FFN_COLS, Dm)
    hb = tm // FFN_HALO
    nhalo = S // FFN_HALO
    te = tm + 2 * FFN_HALO
    blk = lambda b, i: (b, i, 0)
    return pl.pallas_call(
        functools.partial(_ffn_body, tm=tm),
        grid=(B, S // tm),
        in_specs=[pl.BlockSpec((1, tm, Dm), blk),
                  pl.BlockSpec((1, FFN_HALO, Dm), lambda b, i: (b, jnp.minimum((i + 1) * hb, nhalo - 1), 0)),
                  pl.BlockSpec((1, FFN_HALO, Dm), lambda b, i: (b, jnp.maximum(i * hb - 1, 0), 0)),
                  _resident((1, Dm)), _resident(wu3.shape), _resident(cw3.shape), _resident(cb3.shape),
                  _resident(wd3.shape), _resident((1, Dm)),
                  pl.BlockSpec((1, tm, PD), blk),
                  _resident((Dm, Dm)), _resident((PD, Dm))],
        out_specs=pl.BlockSpec((1, tm, Dm), blk),
        out_shape=jax.ShapeDtypeStruct((B, S, Dm), f32),
        scratch_shapes=[pltpu.VMEM((te, Dm), bf16),
                        pltpu.VMEM((2, 2, te, FFN_COLS), f32),
                        pltpu.VMEM((tm, Dm), f32)],
        compiler_params=_cparams(("parallel", "parallel")),
        name="ffn",
    )(x3, x3, x3, g_pre.reshape(1, Dm), wu3, cw3, cb3, wd3, g_post.reshape(1, Dm), p3, w_gate, w_proj)


def _pad_cols(w, n):
    return jnp.pad(w, ((0, 0), (0, n - w.shape[1])))


def _trunk(x, p, norm_mix_pre, norm_mix_post, norm_ffn_pre, norm_ffn_post,
           even_w_in, even_na_rpb, even_ml_gate_b, even_ml_norm_g, even_w_out,
           odd_w_in, odd_gate_up, odd_gate_b, odd_norm_g, odd_w_out,
           ffn_w_up, ffn_conv_w, ffn_conv_b, ffn_w_down, ple_w_proj, ple_w_gate):
    B, S, Dm = x.shape
    T = B * S
    depth = norm_mix_pre.shape[0]
    for i in range(depth):
        j = i // 2
        x2d = x.reshape(T, Dm)
        if i % 2 == 0:
            widths = (3 * NA_WIDTH, 4 * ML_WIDTH, LANE)
            w_in = _pad_cols(even_w_in[j], sum(widths)).astype(bf16)
            a3, m4, g = _norm_matmul(x2d, norm_mix_pre[i], w_in, widths, (bf16, bf16, f32))
            y_na = _neighborhood_attention(a3.reshape(B, S, -1), even_na_rpb[j])
            hf, hb = _mlstm(m4.reshape(B, S, -1), g.reshape(B, S, LANE), even_ml_gate_b[j])
            x2d = _even_out(x2d, y_na.reshape(T, -1), hf.reshape(T, -1), hb.reshape(T, -1), m4,
                            even_ml_norm_g[j], even_w_out[j].astype(bf16), norm_mix_post[i])
        else:
            widths = (2 * GLA_KW, GLA_VW, GLA_VW, LANE)
            w_in = _pad_cols(odd_w_in[j], sum(widths)).astype(bf16)
            qk, v, r, gd = _norm_matmul(x2d, norm_mix_pre[i], w_in, widths, (bf16, bf16, bf16, f32))
            of, ob = _gla(qk.reshape(B, S, -1), v.reshape(B, S, -1), gd.reshape(B, S, LANE),
                          odd_gate_up[j], odd_gate_b[j])
            x2d = _odd_out(x2d, of.reshape(T, -1), ob.reshape(T, -1), r, odd_norm_g[j],
                           odd_w_out[j].astype(bf16), norm_mix_post[i])
        x = _ffn(x2d.reshape(B, S, Dm), norm_ffn_pre[i], ffn_w_up[i].astype(bf16), ffn_conv_w[i],
                 ffn_conv_b[i], ffn_w_down[i].astype(bf16), norm_ffn_post[i], p[i],
                 ple_w_gate[i].astype(bf16), ple_w_proj[i].astype(bf16))
    return x


def kernel(x_prompt, x_sample, p_prompt, p_sample, norm_mix_pre, norm_mix_post, norm_ffn_pre, norm_ffn_post, even_w_in, even_na_rpb, even_ml_gate_b, even_ml_norm_g, even_w_out, odd_w_in, odd_gate_up, odd_gate_b, odd_norm_g, odd_w_out, ffn_w_up, ffn_conv_w, ffn_conv_b, ffn_w_down, ple_w_proj, ple_w_gate):
    params = (norm_mix_pre, norm_mix_post, norm_ffn_pre, norm_ffn_post,
              even_w_in, even_na_rpb, even_ml_gate_b, even_ml_norm_g, even_w_out,
              odd_w_in, odd_gate_up, odd_gate_b, odd_norm_g, odd_w_out,
              ffn_w_up, ffn_conv_w, ffn_conv_b, ffn_w_down, ple_w_proj, ple_w_gate)
    return (_trunk(x_prompt, p_prompt, *params), _trunk(x_sample, p_sample, *params))
```

```python
import functools

import jax
import jax.numpy as jnp
import numpy as np
from jax import lax
from jax.experimental import pallas as pl
from jax.experimental.pallas import tpu as pltpu

f32 = jnp.float32
bf16 = jnp.bfloat16

D_MODEL = 1024
EPS = 1e-6
GRID_W = 64
NA_HEADS = 8
NA_HEAD_DIM = 64
NA_WIN_ROWS = 8
NA_WIN_COLS = 16
NA_GROUP = 4
NA_WIDTH = NA_HEADS * NA_HEAD_DIM
ML_HEADS = 4
ML_HEAD_DIM = 128
ML_WIDTH = ML_HEADS * ML_HEAD_DIM
GLA_HEADS = 4
GLA_DK = 128
GLA_DV = 256
GLA_KW = GLA_HEADS * GLA_DK
GLA_VW = GLA_HEADS * GLA_DV
GLA_GATE_RANK = 16
GLA_TAU = 16.0
CHUNK = 64
D_FF = 2816
LANE = 128
NEG = -1e30
LOG2E = 1.4426950408889634

VMEM_LIMIT = 56 * 1024 * 1024


def _cparams(sem):
    return pltpu.CompilerParams(dimension_semantics=sem, vmem_limit_bytes=VMEM_LIMIT)


def _dot(a, b):
    return jnp.dot(a, b, preferred_element_type=f32)


def _dot_nt(a, b):
    return lax.dot_general(a, b, (((1,), (1,)), ((), ())), preferred_element_type=f32)


def _dot_tn(a, b):
    return lax.dot_general(a, b, (((0,), (0,)), ((), ())), preferred_element_type=f32)


def _dot_f32(a, b):
    return jnp.dot(a, b, precision=lax.Precision.HIGHEST, preferred_element_type=f32)


def _rms(x):
    return x * lax.rsqrt(jnp.mean(x * x, axis=-1, keepdims=True) + EPS)


def _sigmoid(x):
    return 1.0 / (1.0 + jnp.exp(-x))


def _log_sigmoid(x):
    return jnp.minimum(x, 0.0) - jnp.log(1.0 + jnp.exp(-jnp.abs(x)))


def _norm_matmul_body(x_ref, g_ref, w_ref, *refs, widths, with_t):
    hn = (_rms(x_ref[...]) * g_ref[...]).astype(bf16)
    o_refs = refs[1:] if with_t else refs
    off = 0
    for o_ref, n in zip(o_refs, widths):
        for c0 in range(0, n, 512):
            cw = min(512, n - c0)
            o_ref[:, c0:c0 + cw] = _dot(hn, w_ref[:, off + c0:off + c0 + cw]).astype(o_ref.dtype)
        off += n
    if with_t:
        o_refs[len(widths)][...] = _dot_nt(refs[0][...], hn)


def _norm_matmul(x2d, g, w, widths, dtypes, wt=None, tm=512):
    T, Dm = x2d.shape
    N = w.shape[1]
    assert sum(widths) == N and T % tm == 0
    const = lambda i: (0, 0)
    in_specs = [pl.BlockSpec((tm, Dm), lambda i: (i, 0)), pl.BlockSpec((1, Dm), const), pl.BlockSpec((Dm, N), const)]
    out_specs = [pl.BlockSpec((tm, n), lambda i: (i, 0)) for n in widths]
    out_shape = [jax.ShapeDtypeStruct((T, n), dt) for n, dt in zip(widths, dtypes)]
    args = [x2d, g.reshape(1, Dm), w]
    if wt is not None:
        in_specs.append(pl.BlockSpec(wt.shape, const))
        out_specs.append(pl.BlockSpec((wt.shape[0], tm), lambda i: (0, i)))
        out_shape.append(jax.ShapeDtypeStruct((wt.shape[0], T), f32))
        args.append(wt)
    return pl.pallas_call(
        functools.partial(_norm_matmul_body, widths=tuple(widths), with_t=wt is not None),
        grid=(T // tm,),
        in_specs=in_specs,
        out_specs=out_specs,
        out_shape=out_shape,
        compiler_params=_cparams(("parallel",)),
        name="norm_matmul",
    )(*args)


def _na_body(q_ref, k_ref, v_ref, bias_ref, o_ref, *, R, rows):
    i = pl.program_id(2)
    lane = lax.broadcasted_iota(jnp.int32, (GRID_W, LANE), 1)
    first = lane < NA_HEAD_DIM
    nkeys = NA_WIN_ROWS * GRID_W

    def group_body(jg, carry):
        qoffs, kstarts, cases = [], [], []
        for t in range(NA_GROUP):
            j = jg * NA_GROUP + t
            r = i * R + j
            rs = jnp.clip(r - NA_WIN_ROWS // 2, 0, rows - NA_WIN_ROWS)
            cases.append(r - rs)
            qoffs.append(pl.multiple_of(j * GRID_W, GRID_W))
            kstarts.append(pl.multiple_of(rs * GRID_W, GRID_W))
        s_all = []
        for t in range(NA_GROUP):
            q = q_ref[0, pl.ds(qoffs[t], GRID_W), :]
            zero = jnp.zeros_like(q)
            q2 = jnp.concatenate([jnp.where(first, q, zero), jnp.where(first, zero, q)], axis=0)
            kw = k_ref[0, pl.ds(kstarts[t], nkeys), :]
            s = _dot_nt(q2, kw) * (NA_HEAD_DIM ** -0.5 * LOG2E)
            s_all.append(s + jnp.concatenate([bias_ref[cases[t], 0], bias_ref[cases[t], 1]], axis=0))
        p_all, l_all = [], []
        for s in s_all:
            p = jnp.exp2(s - jnp.max(s, axis=-1, keepdims=True))
            l_all.append(jnp.sum(p, axis=-1, keepdims=True))
            p_all.append(p.astype(bf16))
        for t in range(NA_GROUP):
            vw = v_ref[0, pl.ds(kstarts[t], nkeys), :]
            o = _dot(p_all[t], vw) / l_all[t]
            o_ref[0, pl.ds(qoffs[t], GRID_W), :] = jnp.where(
                first, o[:GRID_W], o[GRID_W:]).astype(o_ref.dtype)
        return carry

    lax.fori_loop(0, R // NA_GROUP, group_body, 0)


def _na_bias_table(rpb):
    nr, nc = 2 * NA_WIN_ROWS - 1, 2 * NA_WIN_COLS - 1
    cols = np.arange(GRID_W)
    cs = np.clip(cols - NA_WIN_COLS // 2, 0, GRID_W - NA_WIN_COLS)
    kc = np.arange(GRID_W)
    valid = (kc[None, :] >= cs[:, None]) & (kc[None, :] < cs[:, None] + NA_WIN_COLS)
    dc = kc[None, :] - cols[:, None] + (NA_WIN_COLS - 1)
    expand = (dc[None] == np.arange(nc)[:, None, None]).astype(np.float32).reshape(nc, -1)
    t = jnp.dot(rpb.astype(f32).reshape(-1, nc), expand, precision=lax.Precision.HIGHEST)
    t = t.reshape(NA_HEADS, nr, GRID_W, GRID_W)
    t = jnp.stack([t[:, NA_WIN_ROWS - 1 - off:2 * NA_WIN_ROWS - 1 - off] for off in range(NA_WIN_ROWS)])
    t = jnp.where(valid[None, None, None], t * LOG2E, NEG)
    t = jnp.transpose(t, (0, 1, 3, 2, 4))
    return t.reshape(NA_WIN_ROWS, NA_HEADS, GRID_W, NA_WIN_ROWS * GRID_W)


def _neighborhood_attention(a3, rpb, R=16):
    B, S, _ = a3.shape
    rows = S // GRID_W
    assert rows >= NA_WIN_ROWS and rows % R == 0 and R % NA_GROUP == 0
    bias = _na_bias_table(rpb)
    npair = NA_WIDTH // LANE
    return pl.pallas_call(
        functools.partial(_na_body, R=R, rows=rows),
        grid=(B, npair, rows // R),
        in_specs=[pl.BlockSpec((1, R * GRID_W, LANE), lambda b, p, i: (b, i, p)),
                  pl.BlockSpec((1, S, LANE), lambda b, p, i: (b, 0, npair + p)),
                  pl.BlockSpec((1, S, LANE), lambda b, p, i: (b, 0, 2 * npair + p)),
                  pl.BlockSpec((NA_WIN_ROWS, 2, GRID_W, NA_WIN_ROWS * GRID_W), lambda b, p, i: (0, p, 0, 0))],
        out_specs=pl.BlockSpec((1, R * GRID_W, LANE), lambda b, p, i: (b, i, p)),
        out_shape=jax.ShapeDtypeStruct((B, S, NA_WIDTH), bf16),
        compiler_params=_cparams(("parallel", "parallel", "arbitrary")),
        name="neighborhood_attention",
    )(a3, a3, a3, bias)


def _mlstm_tables():
    L = CHUNK
    t = np.arange(L)
    tril = (t[None, :] <= t[:, None]).astype(np.float32)
    masks = np.stack([tril, tril.T, np.eye(L, dtype=np.float32)])
    rhs = np.zeros((2 * L, 2 * LANE), np.float32)
    rhs[:L, :LANE] = 1.0
    rhs[:L, LANE:] = -1.0
    rhs[L:, LANE:] = 1.0
    return masks, rhs


def _mlstm_body(qf, kf, vf, qb, kb, vb, gtf, gtb, bcol, msk_ref, rhs_ref, hf_ref, hb_ref,
                c_ref, m_ref, *, nch):
    L = CHUNK
    H = ML_HEADS
    dh = ML_HEAD_DIM
    scale = dh ** -0.5
    units = [(d, h) for d in range(2) for h in range(H)]

    @pl.when(pl.program_id(1) == 0)
    def _():
        c_ref[...] = jnp.zeros_like(c_ref)
        m_ref[...] = jnp.zeros_like(m_ref)

    sub = lax.broadcasted_iota(jnp.int32, (4 * H, 1), 0)
    f_row = (sub // H) % 2 == 1

    def chunk(j, carry):
        cjs = (j, nch - 1 - j)
        sts = [pl.multiple_of(cj * L, L) for cj in cjs]
        tril, triu, eye = msk_ref[0], msk_ref[1], msk_ref[2]
        rhs = rhs_ref[...]
        gts, brows = [], []
        for d, gt_ref in enumerate((gtf, gtb)):
            gt = gt_ref[0, cjs[d]] + bcol[...]
            gt = jnp.where(f_row, _log_sigmoid(gt), gt) * LOG2E
            gts.append(gt)
            brows.append(_dot_f32(gt, triu if d == 0 else tril))

        bcols, wcols = [], []
        for d, h in units:
            li_row = gts[d][d * 2 * H + h:d * 2 * H + h + 1, :]
            lf_row = gts[d][d * 2 * H + H + h:d * 2 * H + H + h + 1, :]
            lhs = jnp.concatenate([(tril if d == 0 else triu) * lf_row, eye * li_row], axis=1)
            hi = lhs.astype(bf16)
            lo = (lhs - hi.astype(f32)).astype(bf16)
            p = _dot(hi, rhs) + _dot(lo, rhs)
            bcols.append(p[:, :LANE])
            wcols.append(p[:, LANE:])

        qs, ks, vs, qks = [], [], [], []
        for d, h in units:
            q_ref, k_ref, v_ref = (qf, kf, vf) if d == 0 else (qb, kb, vb)
            hs = slice(h * dh, (h + 1) * dh)
            qs.append(q_ref[0, pl.ds(sts[d], L), hs])
            ks.append(k_ref[0, pl.ds(sts[d], L), hs])
            vs.append(v_ref[0, pl.ds(sts[d], L), hs])
            qks.append(_dot_nt(qs[-1], ks[-1]))

        scs, aiss, mts = [], [], []
        for u, (d, h) in enumerate(units):
            li_row = gts[d][d * 2 * H + h:d * 2 * H + h + 1, :]
            b_row = brows[d][d * 2 * H + H + h:d * 2 * H + H + h + 1, :]
            m_prev = m_ref[u]
            dm = jnp.where((tril if d == 0 else triu) > 0, bcols[u][:, :L] - b_row + li_row, NEG)
            inter = bcols[u] + m_prev
            mt = jnp.maximum(inter, jnp.max(dm, axis=-1, keepdims=True))
            scs.append(qks[u] * scale * jnp.exp2(dm - mt[:, :L]))
            aiss.append(jnp.exp2(inter - mt) * scale)
            mts.append(mt)

        nvs, qcs = [], []
        for u in range(len(units)):
            nvs.append(_dot(scs[u].astype(bf16), vs[u]))
            qcs.append(_dot(qs[u], c_ref[u].astype(bf16)))

        for u, (d, h) in enumerate(units):
            h_ref = hf_ref if d == 0 else hb_ref
            num = nvs[u] + aiss[u] * qcs[u][:, :dh]
            den = jnp.sum(scs[u], axis=-1, keepdims=True) + aiss[u] * qcs[u][:, dh:]
            h_ref[0, pl.ds(sts[d], L), h * dh:(h + 1) * dh] = (
                num / jnp.maximum(jnp.abs(den), jnp.exp2(-mts[u]))).astype(h_ref.dtype)

        for u, (d, h) in enumerate(units):
            m_prev = m_ref[u]
            bl = bcols[u][L - 1:L, :] if d == 0 else bcols[u][0:1, :]
            ds = wcols[u] + bl
            m_new = jnp.maximum(bl + m_prev, jnp.max(ds, axis=0, keepdims=True))
            decay = jnp.exp2(bl + m_prev - m_new)
            ws = jnp.exp2(ds - m_new)
            wv = jnp.concatenate([ws * vs[u].astype(f32), ws], axis=1).astype(bf16)
            c_ref[u] = jnp.concatenate([decay, decay], axis=1) * c_ref[u] + _dot_tn(ks[u], wv)
            m_ref[u] = m_new
        return carry

    lax.fori_loop(0, nch, chunk, 0)


def _mlstm(m4, gt, gate_b, tb=512):
    B, S, _ = m4.shape
    tb = min(tb, S)
    assert S % tb == 0 and tb % CHUNK == 0
    nb = S // tb
    nch = tb // CHUNK
    ng = 4 * ML_HEADS
    bcol = gate_b.astype(f32).reshape(ng, 1)
    masks_np, rhs_np = _mlstm_tables()
    masks = jnp.asarray(masks_np, f32)
    rhs = jnp.asarray(rhs_np, bf16)
    W = ML_WIDTH

    def fwd(c):
        return lambda b, i: (b, i, c)

    def bwd(c):
        return lambda b, i: (b, nb - 1 - i, c)

    in_specs = (
        [pl.BlockSpec((1, tb, W), fwd(c)) for c in range(3)]
        + [pl.BlockSpec((1, tb, W), bwd(c)) for c in range(3)]
        + [pl.BlockSpec((1, nch, ng, CHUNK), lambda b, i: (b, i, 0, 0)),
           pl.BlockSpec((1, nch, ng, CHUNK), lambda b, i: (b, nb - 1 - i, 0, 0)),
           pl.BlockSpec((ng, 1), lambda b, i: (0, 0)),
           pl.BlockSpec(masks.shape, lambda b, i: (0, 0, 0)),
           pl.BlockSpec(rhs.shape, lambda b, i: (0, 0))])
    return pl.pallas_call(
        functools.partial(_mlstm_body, nch=nch),
        grid=(B, nb),
        in_specs=in_specs,
        out_specs=[pl.BlockSpec((1, tb, W), fwd(0)), pl.BlockSpec((1, tb, W), bwd(0))],
        out_shape=[jax.ShapeDtypeStruct((B, S, W), bf16)] * 2,
        scratch_shapes=[pltpu.VMEM((2 * ML_HEADS, ML_HEAD_DIM, 2 * ML_HEAD_DIM), f32),
                        pltpu.VMEM((2 * ML_HEADS, 1, LANE), f32)],
        compiler_params=_cparams(("parallel", "arbitrary")),
        name="mlstm",
    )(m4, m4, m4, m4, m4, m4, gt, gt, bcol, masks, rhs)


def _even_out_body(x_ref, na_ref, hf_ref, hb_ref, o_ref, g_ref, w_ref, gp_ref, out_ref):
    hm = hf_ref[...].astype(f32) + hb_ref[...].astype(f32)
    hn = jnp.concatenate(
        [_rms(hm[:, h * ML_HEAD_DIM:(h + 1) * ML_HEAD_DIM]) for h in range(ML_HEADS)], axis=-1)
    y_ml = _sigmoid(o_ref[...].astype(f32)) * (hn * g_ref[...])
    m = _dot(na_ref[...], w_ref[:NA_WIDTH, :]) + _dot(y_ml.astype(bf16), w_ref[NA_WIDTH:, :])
    out_ref[...] = x_ref[...] + _rms(m) * gp_ref[...]


def _even_out(x2d, na2d, hf2d, hb2d, m4_2d, ml_g, w_out, g_post, tm=512):
    T, Dm = x2d.shape
    W = ML_WIDTH
    row = lambda i: (i, 0)
    const = lambda i: (0, 0)
    return pl.pallas_call(
        _even_out_body,
        grid=(T // tm,),
        in_specs=[pl.BlockSpec((tm, Dm), row), pl.BlockSpec((tm, NA_WIDTH), row),
                  pl.BlockSpec((tm, W), row), pl.BlockSpec((tm, W), row),
                  pl.BlockSpec((tm, W), lambda i: (i, 3)),
                  pl.BlockSpec((1, W), const), pl.BlockSpec((NA_WIDTH + W, Dm), const),
                  pl.BlockSpec((1, Dm), const)],
        out_specs=pl.BlockSpec((tm, Dm), row),
        out_shape=jax.ShapeDtypeStruct((T, Dm), f32),
        compiler_params=_cparams(("parallel",)),
        name="even_out",
    )(x2d, na2d, hf2d, hb2d, m4_2d, ml_g.reshape(1, W), w_out, g_post.reshape(1, Dm))


GLA_BCAST_LEVELS = (32, 16, 8, 4)
T_SGN, T_COEF, T_PAIR = 0, 4, 8


def _gla_tables():
    L = CHUNK
    t = np.arange(L)
    tril = (t[None, :] <= t[:, None]).astype(np.float32)
    cum = np.stack([tril, tril.T])
    tbl = np.zeros((2, 12, L, LANE), np.float32)
    pm = np.zeros((2, 7, L, L), np.float32)
    for lv, g in enumerate((32, 16, 8, 4, 2, 1)):
        a = (t // (2 * g)) * 2 * g
        upper = (t - a) >= g
        same = a[:, None] == a[None, :]
        pm[0, lv] = same & upper[:, None] & (~upper)[None, :]
        pm[1, lv] = same & (~upper)[:, None] & upper[None, :]
        if g >= 4:
            tbl[0, T_SGN + lv] = np.where(upper, 1.0, -1.0)[:, None]
            tbl[1, T_SGN + lv] = np.where(upper, -1.0, 1.0)[:, None]
    pm[:, 6] = np.eye(L)
    r4, r2 = t % 4, t % 2
    tbl[0, T_COEF + 0] = np.isin(r4, (2, 3))[:, None]
    tbl[0, T_COEF + 1] = (r4 == 3)[:, None]
    tbl[0, T_COEF + 2] = (r4 == 0)[:, None]
    tbl[0, T_COEF + 3] = (r2 == 1)[:, None]
    tbl[1, T_COEF + 0] = np.isin(r4, (0, 1))[:, None]
    tbl[1, T_COEF + 1] = (r4 == 3)[:, None]
    tbl[1, T_COEF + 2] = (r4 == 0)[:, None]
    tbl[1, T_COEF + 3] = (r2 == 0)[:, None]
    for p in range(4):
        tbl[:, T_PAIR + p, :, :L] = pm[:, 2 * p]
        if p < 3:
            tbl[:, T_PAIR + p, :, L:] = pm[:, 2 * p + 1]
    return cum, tbl


def _gla_body(qf, kf, vf, gdf, qb, kb, vb, gdb, gu_ref, gbias_ref, cum_ref, tbl_ref,
              of_ref, ob_ref, s_ref, *, nch):
    L = CHUNK
    H = GLA_HEADS
    dk = GLA_DK
    dv = GLA_DV
    scale = dk ** -0.5
    units = [(d, h) for d in range(2) for h in range(H)]

    @pl.when(pl.program_id(1) == 0)
    def _():
        s_ref[...] = jnp.zeros_like(s_ref)

    def chunk(j, carry):
        cjs = (j, nch - 1 - j)
        sts = [pl.multiple_of(cj * L, L) for cj in cjs]
        zk = jnp.zeros((L, dk), bf16)

        las, bs = [], []
        for d, gd_ref in enumerate((gdf, gdb)):
            gd = gd_ref[0, pl.ds(sts[d], L), :].astype(bf16)
            la = _log_sigmoid(_dot(gd, gu_ref[d]) + gbias_ref[d]) * (LOG2E / GLA_TAU)
            hi = la.astype(bf16)
            lo = (la - hi.astype(f32)).astype(bf16)
            las.append(la)
            bs.append(_dot(cum_ref[d], hi) + _dot(cum_ref[d], lo))

        qss, kfs, vs, rs = [], [], [], []
        for d, h in units:
            q_ref, k_ref, v_ref = (qf, kf, vf) if d == 0 else (qb, kb, vb)
            ks = slice(h * dk, (h + 1) * dk)
            la = las[d][:, ks]
            b = bs[d][:, ks]
            k = k_ref[0, pl.ds(sts[d], L), ks]
            qs = q_ref[0, pl.ds(sts[d], L), ks].astype(f32) * scale
            kf32 = k.astype(f32)
            xs = []
            for lv, g in enumerate(GLA_BCAST_LEVELS):
                refs = [a + g - 1 + d for a in range(0, L, 2 * g)]
                bref = jnp.concatenate(
                    [jnp.broadcast_to(b[r:r + 1, :], (2 * g, dk)) for r in refs], axis=0)
                xs.append(jnp.exp2((b - bref) * tbl_ref[d, T_SGN + lv]))
            la_dn = pltpu.roll(la, 1, axis=0)
            la_up = pltpu.roll(la, L - 1, axis=0)
            xs.append(jnp.exp2(la * tbl_ref[d, T_COEF] + la_dn * tbl_ref[d, T_COEF + 1]
                               + la_up * tbl_ref[d, T_COEF + 2]))
            xs.append(jnp.exp2(la * tbl_ref[d, T_COEF + 3]))
            r_u = []
            for p in range(3):
                xa, xb = xs[2 * p], xs[2 * p + 1]
                lhs = jnp.concatenate([(qs * xa).astype(bf16), (qs * xb).astype(bf16)], axis=1)
                rhs = jnp.concatenate(
                    [jnp.concatenate([(kf32 * xa).astype(bf16), zk], axis=1),
                     jnp.concatenate([zk, (kf32 * xb).astype(bf16)], axis=1)], axis=0)
                r_u.append(_dot_nt(lhs, rhs))
            r_u.append(_dot_nt(qs.astype(bf16), jnp.concatenate([k, zk], axis=0)))
            rs.append(r_u)
            qss.append(qs)
            kfs.append(kf32)
            vs.append(v_ref[0, pl.ds(sts[d], L), h * dv:(h + 1) * dv])

        for u, (d, h) in enumerate(units):
            o_ref = of_ref if d == 0 else ob_ref
            b = bs[d][:, h * dk:(h + 1) * dk]
            a2 = jnp.where(tbl_ref[d, T_PAIR + 3] > 0, rs[u][3], 0.0)
            for p in range(2, -1, -1):
                a2 = jnp.where(tbl_ref[d, T_PAIR + p] > 0, rs[u][p], a2)
            v2 = jnp.concatenate([vs[u], vs[u]], axis=0)
            o = _dot(a2.astype(bf16), v2) + _dot_nt((qss[u] * jnp.exp2(b)).astype(bf16),
                                                    s_ref[u].astype(bf16))
            o_ref[0, pl.ds(sts[d], L), h * dv:(h + 1) * dv] = o.astype(o_ref.dtype)

        for u, (d, h) in enumerate(units):
            b = bs[d][:, h * dk:(h + 1) * dk]
            bl = b[L - 1:L, :] if d == 0 else b[0:1, :]
            kt = (kfs[u] * jnp.exp2(bl - b)).astype(bf16)
            s_ref[u] = s_ref[u] * jnp.exp2(bl) + _dot_tn(vs[u], kt)
        return carry

    lax.fori_loop(0, nch, chunk, 0)


def _gla(qk, v, gd, gate_up, gate_b, tb=512):
    B, S, _ = qk.shape
    tb = min(tb, S)
    assert S % tb == 0 and tb % CHUNK == 0
    nb = S // tb
    nch = tb // CHUNK
    R = GLA_GATE_RANK
    gu = jnp.zeros((2, LANE, GLA_KW), f32)
    gu = gu.at[0, :R].set(gate_up[0]).at[1, R:2 * R].set(gate_up[1]).astype(bf16)
    gbias = gate_b.astype(f32).reshape(2, 1, GLA_KW)
    cum_np, tbl_np = _gla_tables()
    cum = jnp.asarray(cum_np, bf16)
    tbl = jnp.asarray(tbl_np, f32)

    def fwd(c):
        return lambda b, i: (b, i, c)

    def bwd(c):
        return lambda b, i: (b, nb - 1 - i, c)

    const3 = lambda b, i: (0, 0, 0)
    in_specs = []
    for mk in (fwd, bwd):
        in_specs += [pl.BlockSpec((1, tb, GLA_KW), mk(0)), pl.BlockSpec((1, tb, GLA_KW), mk(1)),
                     pl.BlockSpec((1, tb, GLA_VW), mk(0)), pl.BlockSpec((1, tb, LANE), mk(0))]
    in_specs += [pl.BlockSpec((2, LANE, GLA_KW), const3), pl.BlockSpec((2, 1, GLA_KW), const3),
                 pl.BlockSpec(cum.shape, const3), pl.BlockSpec(tbl.shape, lambda b, i: (0, 0, 0, 0))]
    return pl.pallas_call(
        functools.partial(_gla_body, nch=nch),
        grid=(B, nb),
        in_specs=in_specs,
        out_specs=[pl.BlockSpec((1, tb, GLA_VW), fwd(0)), pl.BlockSpec((1, tb, GLA_VW), bwd(0))],
        out_shape=[jax.ShapeDtypeStruct((B, S, GLA_VW), bf16)] * 2,
        scratch_shapes=[pltpu.VMEM((2 * GLA_HEADS, GLA_DV, GLA_DK), f32)],
        compiler_params=_cparams(("parallel", "arbitrary")),
        name="gla",
    )(qk, qk, v, gd, qk, qk, v, gd, gu, gbias, cum, tbl)


def _odd_out_body(x_ref, of_ref, ob_ref, r_ref, g_ref, w_ref, gp_ref, out_ref):
    o = of_ref[...].astype(f32) + ob_ref[...].astype(f32)
    hn = jnp.concatenate(
        [_rms(o[:, h * GLA_DV:(h + 1) * GLA_DV]) for h in range(GLA_HEADS)], axis=-1)
    r = r_ref[...].astype(f32)
    y = (hn * g_ref[...]) * (r * _sigmoid(r))
    m = _dot(y.astype(bf16), w_ref[...])
    out_ref[...] = x_ref[...] + _rms(m) * gp_ref[...]


def _odd_out(x2d, of2d, ob2d, r2d, norm_g, w_out, g_post, tm=512):
    T, Dm = x2d.shape
    row = lambda i: (i, 0)
    const = lambda i: (0, 0)
    return pl.pallas_call(
        _odd_out_body,
        grid=(T // tm,),
        in_specs=[pl.BlockSpec((tm, Dm), row), pl.BlockSpec((tm, GLA_VW), row),
                  pl.BlockSpec((tm, GLA_VW), row), pl.BlockSpec((tm, GLA_VW), row),
                  pl.BlockSpec((1, GLA_VW), const), pl.BlockSpec((GLA_VW, Dm), const),
                  pl.BlockSpec((1, Dm), const)],
        out_specs=pl.BlockSpec((tm, Dm), row),
        out_shape=jax.ShapeDtypeStruct((T, Dm), f32),
        compiler_params=_cparams(("parallel",)),
        name="odd_out",
    )(x2d, of2d, ob2d, r2d, norm_g.reshape(1, GLA_VW), w_out, g_post.reshape(1, Dm))


FFN_HALO = 8
FFN_COLS = 256
GELU_K2 = 2.0 * 0.7978845608028654
GELU_C = 0.044715


def _ffn_body(x_ref, xn_ref, xp_ref, gpre_ref, wu_ref, cw_ref, cb_ref, wd_ref, gpost_ref, p_ref,
              wg_ref, wp_ref, out_ref, lhs_ref, ue_ref, acc_ref, *, tm):
    i = pl.program_id(1)
    keep_prev = (i > 0).astype(f32)
    keep_next = (i < pl.num_programs(1) - 1).astype(f32)
    gpre = gpre_ref[...]
    x = x_ref[0]
    halo = jnp.concatenate([_rms(xn_ref[0]) * gpre * keep_next, _rms(xp_ref[0]) * gpre * keep_prev], axis=0)
    lhs_ref[...] = jnp.concatenate([(_rms(x) * gpre).astype(bf16), halo.astype(bf16)], axis=0)
    acc_ref[...] = jnp.zeros_like(acc_ref)
    te = tm + 2 * FFN_HALO
    nchunks = D_FF // FFN_COLS

    def cols(base, n):
        return pl.ds(pl.multiple_of(base + n * FFN_COLS, FFN_COLS), FFN_COLS)

    def up(n, slot):
        lhs = lhs_ref[...]
        ue_ref[slot, 0] = _dot(lhs, wu_ref[:, cols(D_FF, n)])
        ue_ref[slot, 1] = _dot(lhs, wu_ref[:, cols(0, n)])

    def conv(n, slot, half):
        ue = ue_ref[slot, half]
        cs = cols(D_FF * (1 - half), n)
        w = cw_ref[:, cs]
        above = pltpu.roll(ue, 1, axis=0)[:tm]
        below = pltpu.roll(ue, te - 1, axis=0)[:tm]
        return above * w[0:1] + ue[:tm] * w[1:2] + below * w[2:3] + cb_ref[:, cs]

    def down(n, slot):
        g = conv(n, slot, 0)
        e = jnp.exp2(g * (g * g * (-GELU_K2 * GELU_C * LOG2E) + (-GELU_K2 * LOG2E)))
        act = (g / (1.0 + e) * conv(n, slot, 1)).astype(bf16)
        acc_ref[...] += _dot(act, wd_ref[cols(0, n), :])

    up(0, 0)

    def pair(jj, carry):
        n = 2 * jj
        up(n + 1, 1)
        down(n, 0)
        up(n + 2, 0)
        down(n + 1, 1)
        return carry

    lax.fori_loop(0, (nchunks - 1) // 2, pair, 0)
    down(nchunks - 1, 0)
    x2 = x + _rms(acc_ref[...]) * gpost_ref[...]
    gate = _sigmoid(_dot(x2.astype(bf16), wg_ref[...]))
    out_ref[0] = x2 + _dot(p_ref[0].astype(bf16), wp_ref[...]) * gate


def _resident(shape):
    nd = len(shape)
    return pl.BlockSpec(shape, lambda b, i: (0,) * nd, pipeline_mode=pl.Buffered(1))


def _ffn(x3, g_pre, w_up, conv_w, conv_b, w_down, g_post, p3, w_gate, w_proj, tm=512):
    B, S, Dm = x3.shape
    NF = w_up.shape[1]
    PD = p3.shape[-1]
    tm = min(tm, S)
    nchunks = D_FF // FFN_COLS
    assert S % tm == 0 and tm % FFN_HALO == 0 and D_FF % FFN_COLS == 0 and NF == 2 * D_FF and nchunks % 2 == 1
    hb = tm // FFN_HALO
    nhalo = S // FFN_HALO
    te = tm + 2 * FFN_HALO
    blk = lambda b, i: (b, i, 0)
    return pl.pallas_call(
        functools.partial(_ffn_body, tm=tm),
        grid=(B, S // tm),
        in_specs=[pl.BlockSpec((1, tm, Dm), blk),
                  pl.BlockSpec((1, FFN_HALO, Dm), lambda b, i: (b, jnp.minimum((i + 1) * hb, nhalo - 1), 0)),
                  pl.BlockSpec((1, FFN_HALO, Dm), lambda b, i: (b, jnp.maximum(i * hb - 1, 0), 0)),
                  _resident((1, Dm)), _resident((Dm, NF)), _resident((3, NF)), _resident((1, NF)),
                  _resident((D_FF, Dm)), _resident((1, Dm)),
                  pl.BlockSpec((1, tm, PD), blk),
                  _resident((Dm, Dm)), _resident((PD, Dm))],
        out_specs=pl.BlockSpec((1, tm, Dm), blk),
        out_shape=jax.ShapeDtypeStruct((B, S, Dm), f32),
        scratch_shapes=[pltpu.VMEM((te, Dm), bf16),
                        pltpu.VMEM((2, 2, te, FFN_COLS), f32),
                        pltpu.VMEM((tm, Dm), f32)],
        compiler_params=_cparams(("parallel", "parallel")),
        name="ffn",
    )(x3, x3, x3, g_pre.reshape(1, Dm), w_up, conv_w, conv_b.reshape(1, NF), w_down,
      g_post.reshape(1, Dm), p3, w_gate, w_proj)


def _pad_cols(w, n):
    return jnp.pad(w, ((0, 0), (0, n - w.shape[1])))


def _trunk(x, p, norm_mix_pre, norm_mix_post, norm_ffn_pre, norm_ffn_post,
           even_w_in, even_na_rpb, even_ml_gate_b, even_ml_norm_g, even_w_out,
           odd_w_in, odd_gate_up, odd_gate_b, odd_norm_g, odd_w_out,
           ffn_w_up, ffn_conv_w, ffn_conv_b, ffn_w_down, ple_w_proj, ple_w_gate):
    B, S, Dm = x.shape
    T = B * S
    depth = norm_mix_pre.shape[0]
    for i in range(depth):
        j = i // 2
        x2d = x.reshape(T, Dm)
        if i % 2 == 0:
            widths = (3 * NA_WIDTH, 4 * ML_WIDTH)
            w_in = even_w_in[j].astype(bf16)
            a3, m4, gt = _norm_matmul(x2d, norm_mix_pre[i], w_in[:, :sum(widths)], widths, (bf16, bf16),
                                      wt=w_in[:, sum(widths):].T)
            y_na = _neighborhood_attention(a3.reshape(B, S, -1), even_na_rpb[j])
            gt = jnp.transpose(gt.reshape(-1, B, S // CHUNK, CHUNK), (1, 2, 0, 3))
            hf, hb = _mlstm(m4.reshape(B, S, -1), gt, even_ml_gate_b[j])
            x2d = _even_out(x2d, y_na.reshape(T, -1), hf.reshape(T, -1), hb.reshape(T, -1), m4,
                            even_ml_norm_g[j], even_w_out[j].astype(bf16), norm_mix_post[i])
        else:
            widths = (2 * GLA_KW, GLA_VW, GLA_VW, LANE)
            w_in = _pad_cols(odd_w_in[j], sum(widths)).astype(bf16)
            qk, v, r, gd = _norm_matmul(x2d, norm_mix_pre[i], w_in, widths, (bf16, bf16, bf16, f32))
            of, ob = _gla(qk.reshape(B, S, -1), v.reshape(B, S, -1), gd.reshape(B, S, LANE),
                          odd_gate_up[j], odd_gate_b[j])
            x2d = _odd_out(x2d, of.reshape(T, -1), ob.reshape(T, -1), r, odd_norm_g[j],
                           odd_w_out[j].astype(bf16), norm_mix_post[i])
        x = _ffn(x2d.reshape(B, S, Dm), norm_ffn_pre[i], ffn_w_up[i].astype(bf16), ffn_conv_w[i],
                 ffn_conv_b[i], ffn_w_down[i].astype(bf16), norm_ffn_post[i], p[i],
                 ple_w_gate[i].astype(bf16), ple_w_proj[i].astype(bf16))
    return x


def kernel(x_prompt, x_sample, p_prompt, p_sample, norm_mix_pre, norm_mix_post, norm_ffn_pre, norm_ffn_post, even_w_in, even_na_rpb, even_ml_gate_b, even_ml_norm_g, even_w_out, odd_w_in, odd_gate_up, odd_gate_b, odd_norm_g, odd_w_out, ffn_w_up, ffn_conv_w, ffn_conv_b, ffn_w_down, ple_w_proj, ple_w_gate):
    params = (norm_mix_pre, norm_mix_post, norm_ffn_pre, norm_ffn_post,
              even_w_in, even_na_rpb, even_ml_gate_b, even_ml_norm_g, even_w_out,
              odd_w_in, odd_gate_up, odd_gate_b, odd_norm_g, odd_w_out,
              ffn_w_up, ffn_conv_w, ffn_conv_b, ffn_w_down, ple_w_proj, ple_w_gate)
    return (_trunk(x_prompt, p_prompt, *params), _trunk(x_sample, p_sample, *params))
```

```python
import functools

import jax
import jax.numpy as jnp
import numpy as np
from jax import lax
from jax.experimental import pallas as pl
from jax.experimental.pallas import tpu as pltpu

f32 = jnp.float32
bf16 = jnp.bfloat16

D_MODEL = 1024
EPS = 1e-6
GRID_W = 64
NA_HEADS = 8
NA_HEAD_DIM = 64
NA_WIN_ROWS = 8
NA_WIN_COLS = 16
NA_GROUP = 4
NA_WIDTH = NA_HEADS * NA_HEAD_DIM
ML_HEADS = 4
ML_HEAD_DIM = 128
ML_WIDTH = ML_HEADS * ML_HEAD_DIM
GLA_HEADS = 4
GLA_DK = 128
GLA_DV = 256
GLA_KW = GLA_HEADS * GLA_DK
GLA_VW = GLA_HEADS * GLA_DV
GLA_GATE_RANK = 16
GLA_TAU = 16.0
CHUNK = 64
D_FF = 2816
LANE = 128
NEG = -1e30
LOG2E = 1.4426950408889634

VMEM_LIMIT = 56 * 1024 * 1024


def _cparams(sem):
    return pltpu.CompilerParams(dimension_semantics=sem, vmem_limit_bytes=VMEM_LIMIT)


def _dot(a, b):
    return jnp.dot(a, b, preferred_element_type=f32)


def _dot_nt(a, b):
    return lax.dot_general(a, b, (((1,), (1,)), ((), ())), preferred_element_type=f32)


def _dot_tn(a, b):
    return lax.dot_general(a, b, (((0,), (0,)), ((), ())), preferred_element_type=f32)


def _dot_f32(a, b):
    return jnp.dot(a, b, precision=lax.Precision.HIGHEST, preferred_element_type=f32)


def _rms(x):
    return x * lax.rsqrt(jnp.mean(x * x, axis=-1, keepdims=True) + EPS)


def _sigmoid(x):
    return 1.0 / (1.0 + jnp.exp(-x))


def _log_sigmoid(x):
    return jnp.minimum(x, 0.0) - jnp.log(1.0 + jnp.exp(-jnp.abs(x)))


def _norm_matmul_body(x_ref, g_ref, w_ref, *refs, widths, with_t):
    hn = (_rms(x_ref[...]) * g_ref[...]).astype(bf16)
    o_refs = refs[1:] if with_t else refs
    off = 0
    for o_ref, n in zip(o_refs, widths):
        for c0 in range(0, n, 512):
            cw = min(512, n - c0)
            o_ref[:, c0:c0 + cw] = _dot(hn, w_ref[:, off + c0:off + c0 + cw]).astype(o_ref.dtype)
        off += n
    if with_t:
        o_refs[len(widths)][...] = _dot_nt(refs[0][...], hn)


def _norm_matmul(x2d, g, w, widths, dtypes, wt=None, tm=512):
    T, Dm = x2d.shape
    N = w.shape[1]
    assert sum(widths) == N and T % tm == 0
    const = lambda i: (0, 0)
    in_specs = [pl.BlockSpec((tm, Dm), lambda i: (i, 0)), pl.BlockSpec((1, Dm), const), pl.BlockSpec((Dm, N), const)]
    out_specs = [pl.BlockSpec((tm, n), lambda i: (i, 0)) for n in widths]
    out_shape = [jax.ShapeDtypeStruct((T, n), dt) for n, dt in zip(widths, dtypes)]
    args = [x2d, g.reshape(1, Dm), w]
    if wt is not None:
        in_specs.append(pl.BlockSpec(wt.shape, const))
        out_specs.append(pl.BlockSpec((wt.shape[0], tm), lambda i: (0, i)))
        out_shape.append(jax.ShapeDtypeStruct((wt.shape[0], T), f32))
        args.append(wt)
    return pl.pallas_call(
        functools.partial(_norm_matmul_body, widths=tuple(widths), with_t=wt is not None),
        grid=(T // tm,),
        in_specs=in_specs,
        out_specs=out_specs,
        out_shape=out_shape,
        compiler_params=_cparams(("parallel",)),
        name="norm_matmul",
    )(*args)


def _na_body(q_ref, k_ref, v_ref, bias_ref, o_ref, *, R, rows):
    i = pl.program_id(2)
    lane = lax.broadcasted_iota(jnp.int32, (GRID_W, LANE), 1)
    first = lane < NA_HEAD_DIM
    nkeys = NA_WIN_ROWS * GRID_W

    def group_body(jg, carry):
        qoffs, kstarts, cases = [], [], []
        for t in range(NA_GROUP):
            j = jg * NA_GROUP + t
            r = i * R + j
            rs = jnp.clip(r - NA_WIN_ROWS // 2, 0, rows - NA_WIN_ROWS)
            cases.append(r - rs)
            qoffs.append(pl.multiple_of(j * GRID_W, GRID_W))
            kstarts.append(pl.multiple_of(rs * GRID_W, GRID_W))
        s_all = []
        for t in range(NA_GROUP):
            q = q_ref[0, pl.ds(qoffs[t], GRID_W), :]
            zero = jnp.zeros_like(q)
            q2 = jnp.concatenate([jnp.where(first, q, zero), jnp.where(first, zero, q)], axis=0)
            kw = k_ref[0, pl.ds(kstarts[t], nkeys), :]
            s = _dot_nt(q2, kw) * (NA_HEAD_DIM ** -0.5 * LOG2E)
            s_all.append(s + jnp.concatenate([bias_ref[cases[t], 0], bias_ref[cases[t], 1]], axis=0))
        p_all, l_all = [], []
        for s in s_all:
            p = jnp.exp2(s - jnp.max(s, axis=-1, keepdims=True))
            l_all.append(jnp.sum(p, axis=-1, keepdims=True))
            p_all.append(p.astype(bf16))
        for t in range(NA_GROUP):
            vw = v_ref[0, pl.ds(kstarts[t], nkeys), :]
            o = _dot(p_all[t], vw) / l_all[t]
            o_ref[0, pl.ds(qoffs[t], GRID_W), :] = jnp.where(
                first, o[:GRID_W], o[GRID_W:]).astype(o_ref.dtype)
        return carry

    lax.fori_loop(0, R // NA_GROUP, group_body, 0)


def _na_bias_table(rpb):
    nr, nc = 2 * NA_WIN_ROWS - 1, 2 * NA_WIN_COLS - 1
    cols = np.arange(GRID_W)
    cs = np.clip(cols - NA_WIN_COLS // 2, 0, GRID_W - NA_WIN_COLS)
    kc = np.arange(GRID_W)
    valid = (kc[None, :] >= cs[:, None]) & (kc[None, :] < cs[:, None] + NA_WIN_COLS)
    dc = kc[None, :] - cols[:, None] + (NA_WIN_COLS - 1)
    expand = (dc[None] == np.arange(nc)[:, None, None]).astype(np.float32).reshape(nc, -1)
    t = jnp.dot(rpb.astype(f32).reshape(-1, nc), expand, precision=lax.Precision.HIGHEST)
    t = t.reshape(NA_HEADS, nr, GRID_W, GRID_W)
    t = jnp.stack([t[:, NA_WIN_ROWS - 1 - off:2 * NA_WIN_ROWS - 1 - off] for off in range(NA_WIN_ROWS)])
    t = jnp.where(valid[None, None, None], t * LOG2E, NEG)
    t = jnp.transpose(t, (0, 1, 3, 2, 4))
    return t.reshape(NA_WIN_ROWS, NA_HEADS, GRID_W, NA_WIN_ROWS * GRID_W)


def _neighborhood_attention(a3, rpb, R=16):
    B, S, _ = a3.shape
    rows = S // GRID_W
    assert rows >= NA_WIN_ROWS and rows % R == 0 and R % NA_GROUP == 0
    bias = _na_bias_table(rpb)
    npair = NA_WIDTH // LANE
    return pl.pallas_call(
        functools.partial(_na_body, R=R, rows=rows),
        grid=(B, npair, rows // R),
        in_specs=[pl.BlockSpec((1, R * GRID_W, LANE), lambda b, p, i: (b, i, p)),
                  pl.BlockSpec((1, S, LANE), lambda b, p, i: (b, 0, npair + p)),
                  pl.BlockSpec((1, S, LANE), lambda b, p, i: (b, 0, 2 * npair + p)),
                  pl.BlockSpec((NA_WIN_ROWS, 2, GRID_W, NA_WIN_ROWS * GRID_W), lambda b, p, i: (0, p, 0, 0))],
        out_specs=pl.BlockSpec((1, R * GRID_W, LANE), lambda b, p, i: (b, i, p)),
        out_shape=jax.ShapeDtypeStruct((B, S, NA_WIDTH), bf16),
        compiler_params=_cparams(("parallel", "parallel", "arbitrary")),
        name="neighborhood_attention",
    )(a3, a3, a3, bias)


def _mlstm_tables():
    L = CHUNK
    t = np.arange(L)
    tril = (t[None, :] <= t[:, None]).astype(np.float32)
    masks = np.stack([tril, tril.T, np.eye(L, dtype=np.float32)])
    rhs = np.zeros((2 * L, 2 * LANE), np.float32)
    rhs[:L, :LANE] = 1.0
    rhs[:L, LANE:] = -1.0
    rhs[L:, LANE:] = 1.0
    return masks, rhs


def _mlstm_body(qf, kf, vf, qb, kb, vb, gtf, gtb, bcol, msk_ref, rhs_ref, hf_ref, hb_ref,
                c_ref, m_ref, *, nch):
    L = CHUNK
    H = ML_HEADS
    dh = ML_HEAD_DIM
    scale = dh ** -0.5
    units = [(d, h) for d in range(2) for h in range(H)]

    @pl.when(pl.program_id(1) == 0)
    def _():
        c_ref[...] = jnp.zeros_like(c_ref)
        m_ref[...] = jnp.zeros_like(m_ref)

    sub = lax.broadcasted_iota(jnp.int32, (4 * H, 1), 0)
    f_row = (sub // H) % 2 == 1

    def chunk(j, carry):
        cjs = (j, nch - 1 - j)
        sts = [pl.multiple_of(cj * L, L) for cj in cjs]
        tril, triu, eye = msk_ref[0], msk_ref[1], msk_ref[2]
        rhs = rhs_ref[...]
        gts, brows = [], []
        for d, gt_ref in enumerate((gtf, gtb)):
            gt = gt_ref[0, cjs[d]] + bcol[...]
            gt = jnp.where(f_row, _log_sigmoid(gt), gt) * LOG2E
            gts.append(gt)
            brows.append(_dot_f32(gt, triu if d == 0 else tril))

        bcols, wcols = [], []
        for d, h in units:
            li_row = gts[d][d * 2 * H + h:d * 2 * H + h + 1, :]
            lf_row = gts[d][d * 2 * H + H + h:d * 2 * H + H + h + 1, :]
            lhs = jnp.concatenate([(tril if d == 0 else triu) * lf_row, eye * li_row], axis=1)
            hi = lhs.astype(bf16)
            lo = (lhs - hi.astype(f32)).astype(bf16)
            p = _dot(hi, rhs) + _dot(lo, rhs)
            bcols.append(p[:, :LANE])
            wcols.append(p[:, LANE:])

        qs, ks, vs, qks = [], [], [], []
        for d, h in units:
            q_ref, k_ref, v_ref = (qf, kf, vf) if d == 0 else (qb, kb, vb)
            hs = slice(h * dh, (h + 1) * dh)
            qs.append(q_ref[0, pl.ds(sts[d], L), hs])
            ks.append(k_ref[0, pl.ds(sts[d], L), hs])
            vs.append(v_ref[0, pl.ds(sts[d], L), hs])
            qks.append(_dot_nt(qs[-1], ks[-1]))

        scs, aiss, mts = [], [], []
        for u, (d, h) in enumerate(units):
            li_row = gts[d][d * 2 * H + h:d * 2 * H + h + 1, :]
            b_row = brows[d][d * 2 * H + H + h:d * 2 * H + H + h + 1, :]
            m_prev = m_ref[u]
            dm = jnp.where((tril if d == 0 else triu) > 0, bcols[u][:, :L] - b_row + li_row, NEG)
            inter = bcols[u] + m_prev
            mt = jnp.maximum(inter, jnp.max(dm, axis=-1, keepdims=True))
            scs.append(qks[u] * scale * jnp.exp2(dm - mt[:, :L]))
            aiss.append(jnp.exp2(inter - mt) * scale)
            mts.append(mt)

        nvs, qcs = [], []
        for u in range(len(units)):
            nvs.append(_dot(scs[u].astype(bf16), vs[u]))
            qcs.append(_dot(qs[u], c_ref[u].astype(bf16)))

        for u, (d, h) in enumerate(units):
            h_ref = hf_ref if d == 0 else hb_ref
            num = nvs[u] + aiss[u] * qcs[u][:, :dh]
            den = jnp.sum(scs[u], axis=-1, keepdims=True) + aiss[u] * qcs[u][:, dh:]
            h_ref[0, pl.ds(sts[d], L), h * dh:(h + 1) * dh] = (
                num / jnp.maximum(jnp.abs(den), jnp.exp2(-mts[u]))).astype(h_ref.dtype)

        for u, (d, h) in enumerate(units):
            m_prev = m_ref[u]
            bl = bcols[u][L - 1:L, :] if d == 0 else bcols[u][0:1, :]
            ds = wcols[u] + bl
            m_new = jnp.maximum(bl + m_prev, jnp.max(ds, axis=0, keepdims=True))
            decay = jnp.exp2(bl + m_prev - m_new)
            ws = jnp.exp2(ds - m_new)
            wv = jnp.concatenate([ws * vs[u].astype(f32), ws], axis=1).astype(bf16)
            c_ref[u] = jnp.concatenate([decay, decay], axis=1) * c_ref[u] + _dot_tn(ks[u], wv)
            m_ref[u] = m_new
        return carry

    lax.fori_loop(0, nch, chunk, 0, unroll=True)


def _mlstm(m4, gt, gate_b, tb=512):
    B, S, _ = m4.shape
    tb = min(tb, S)
    assert S % tb == 0 and tb % CHUNK == 0
    nb = S // tb
    nch = tb // CHUNK
    ng = 4 * ML_HEADS
    bcol = gate_b.astype(f32).reshape(ng, 1)
    masks_np, rhs_np = _mlstm_tables()
    masks = jnp.asarray(masks_np, f32)
    rhs = jnp.asarray(rhs_np, bf16)
    W = ML_WIDTH

    def fwd(c):
        return lambda b, i: (b, i, c)

    def bwd(c):
        return lambda b, i: (b, nb - 1 - i, c)

    in_specs = (
        [pl.BlockSpec((1, tb, W), fwd(c)) for c in range(3)]
        + [pl.BlockSpec((1, tb, W), bwd(c)) for c in range(3)]
        + [pl.BlockSpec((1, nch, ng, CHUNK), lambda b, i: (b, i, 0, 0)),
           pl.BlockSpec((1, nch, ng, CHUNK), lambda b, i: (b, nb - 1 - i, 0, 0)),
           pl.BlockSpec((ng, 1), lambda b, i: (0, 0)),
           pl.BlockSpec(masks.shape, lambda b, i: (0, 0, 0)),
           pl.BlockSpec(rhs.shape, lambda b, i: (0, 0))])
    return pl.pallas_call(
        functools.partial(_mlstm_body, nch=nch),
        grid=(B, nb),
        in_specs=in_specs,
        out_specs=[pl.BlockSpec((1, tb, W), fwd(0)), pl.BlockSpec((1, tb, W), bwd(0))],
        out_shape=[jax.ShapeDtypeStruct((B, S, W), bf16)] * 2,
        scratch_shapes=[pltpu.VMEM((2 * ML_HEADS, ML_HEAD_DIM, 2 * ML_HEAD_DIM), f32),
                        pltpu.VMEM((2 * ML_HEADS, 1, LANE), f32)],
        compiler_params=_cparams(("parallel", "arbitrary")),
        name="mlstm",
    )(m4, m4, m4, m4, m4, m4, gt, gt, bcol, masks, rhs)


def _even_out_body(x_ref, na_ref, hf_ref, hb_ref, o_ref, g_ref, w_ref, gp_ref, out_ref):
    hm = hf_ref[...].astype(f32) + hb_ref[...].astype(f32)
    hn = jnp.concatenate(
        [_rms(hm[:, h * ML_HEAD_DIM:(h + 1) * ML_HEAD_DIM]) for h in range(ML_HEADS)], axis=-1)
    y_ml = _sigmoid(o_ref[...].astype(f32)) * (hn * g_ref[...])
    m = _dot(na_ref[...], w_ref[:NA_WIDTH, :]) + _dot(y_ml.astype(bf16), w_ref[NA_WIDTH:, :])
    out_ref[...] = x_ref[...] + _rms(m) * gp_ref[...]


def _even_out(x2d, na2d, hf2d, hb2d, m4_2d, ml_g, w_out, g_post, tm=512):
    T, Dm = x2d.shape
    W = ML_WIDTH
    row = lambda i: (i, 0)
    const = lambda i: (0, 0)
    return pl.pallas_call(
        _even_out_body,
        grid=(T // tm,),
        in_specs=[pl.BlockSpec((tm, Dm), row), pl.BlockSpec((tm, NA_WIDTH), row),
                  pl.BlockSpec((tm, W), row), pl.BlockSpec((tm, W), row),
                  pl.BlockSpec((tm, W), lambda i: (i, 3)),
                  pl.BlockSpec((1, W), const), pl.BlockSpec((NA_WIDTH + W, Dm), const),
                  pl.BlockSpec((1, Dm), const)],
        out_specs=pl.BlockSpec((tm, Dm), row),
        out_shape=jax.ShapeDtypeStruct((T, Dm), f32),
        compiler_params=_cparams(("parallel",)),
        name="even_out",
    )(x2d, na2d, hf2d, hb2d, m4_2d, ml_g.reshape(1, W), w_out, g_post.reshape(1, Dm))


GLA_BCAST_LEVELS = (32, 16, 8, 4)
T_SGN, T_COEF, T_PAIR = 0, 4, 8


def _gla_tables():
    L = CHUNK
    t = np.arange(L)
    tril = (t[None, :] <= t[:, None]).astype(np.float32)
    cum = np.stack([tril, tril.T])
    tbl = np.zeros((2, 12, L, LANE), np.float32)
    pm = np.zeros((2, 7, L, L), np.float32)
    for lv, g in enumerate((32, 16, 8, 4, 2, 1)):
        a = (t // (2 * g)) * 2 * g
        upper = (t - a) >= g
        same = a[:, None] == a[None, :]
        pm[0, lv] = same & upper[:, None] & (~upper)[None, :]
        pm[1, lv] = same & (~upper)[:, None] & upper[None, :]
        if g >= 4:
            tbl[0, T_SGN + lv] = np.where(upper, 1.0, -1.0)[:, None]
            tbl[1, T_SGN + lv] = np.where(upper, -1.0, 1.0)[:, None]
    pm[:, 6] = np.eye(L)
    r4, r2 = t % 4, t % 2
    tbl[0, T_COEF + 0] = np.isin(r4, (2, 3))[:, None]
    tbl[0, T_COEF + 1] = (r4 == 3)[:, None]
    tbl[0, T_COEF + 2] = (r4 == 0)[:, None]
    tbl[0, T_COEF + 3] = (r2 == 1)[:, None]
    tbl[1, T_COEF + 0] = np.isin(r4, (0, 1))[:, None]
    tbl[1, T_COEF + 1] = (r4 == 3)[:, None]
    tbl[1, T_COEF + 2] = (r4 == 0)[:, None]
    tbl[1, T_COEF + 3] = (r2 == 0)[:, None]
    for p in range(4):
        tbl[:, T_PAIR + p, :, :L] = pm[:, 2 * p]
        if p < 3:
            tbl[:, T_PAIR + p, :, L:] = pm[:, 2 * p + 1]
    return cum, tbl


def _gla_body(qf, kf, vf, gdf, qb, kb, vb, gdb, gu_ref, gbias_ref, cum_ref, tbl_ref,
              of_ref, ob_ref, s_ref, *, nch):
    L = CHUNK
    H = GLA_HEADS
    dk = GLA_DK
    dv = GLA_DV
    scale = dk ** -0.5
    units = [(d, h) for d in range(2) for h in range(H)]

    @pl.when(pl.program_id(1) == 0)
    def _():
        s_ref[...] = jnp.zeros_like(s_ref)

    def chunk(j, carry):
        cjs = (j, nch - 1 - j)
        sts = [pl.multiple_of(cj * L, L) for cj in cjs]
        zk = jnp.zeros((L, dk), bf16)

        las, bs = [], []
        for d, gd_ref in enumerate((gdf, gdb)):
            gd = gd_ref[0, pl.ds(sts[d], L), :].astype(bf16)
            la = _log_sigmoid(_dot(gd, gu_ref[d]) + gbias_ref[d]) * (LOG2E / GLA_TAU)
            hi = la.astype(bf16)
            lo = (la - hi.astype(f32)).astype(bf16)
            las.append(la)
            bs.append(_dot(cum_ref[d], hi) + _dot(cum_ref[d], lo))

        qss, kfs, vs, rs = [], [], [], []
        for d, h in units:
            q_ref, k_ref, v_ref = (qf, kf, vf) if d == 0 else (qb, kb, vb)
            ks = slice(h * dk, (h + 1) * dk)
            la = las[d][:, ks]
            b = bs[d][:, ks]
            k = k_ref[0, pl.ds(sts[d], L), ks]
            qs = q_ref[0, pl.ds(sts[d], L), ks].astype(f32) * scale
            kf32 = k.astype(f32)
            xs = []
            for lv, g in enumerate(GLA_BCAST_LEVELS):
                refs = [a + g - 1 + d for a in range(0, L, 2 * g)]
                bref = jnp.concatenate(
                    [jnp.broadcast_to(b[r:r + 1, :], (2 * g, dk)) for r in refs], axis=0)
                xs.append(jnp.exp2((b - bref) * tbl_ref[d, T_SGN + lv]))
            la_dn = pltpu.roll(la, 1, axis=0)
            la_up = pltpu.roll(la, L - 1, axis=0)
            xs.append(jnp.exp2(la * tbl_ref[d, T_COEF] + la_dn * tbl_ref[d, T_COEF + 1]
                               + la_up * tbl_ref[d, T_COEF + 2]))
            xs.append(jnp.exp2(la * tbl_ref[d, T_COEF + 3]))
            r_u = []
            for p in range(3):
                xa, xb = xs[2 * p], xs[2 * p + 1]
                lhs = jnp.concatenate([(qs * xa).astype(bf16), (qs * xb).astype(bf16)], axis=1)
                rhs = jnp.concatenate(
                    [jnp.concatenate([(kf32 * xa).astype(bf16), zk], axis=1),
                     jnp.concatenate([zk, (kf32 * xb).astype(bf16)], axis=1)], axis=0)
                r_u.append(_dot_nt(lhs, rhs))
            r_u.append(_dot_nt(qs.astype(bf16), jnp.concatenate([k, zk], axis=0)))
            rs.append(r_u)
            qss.append(qs)
            kfs.append(kf32)
            vs.append(v_ref[0, pl.ds(sts[d], L), h * dv:(h + 1) * dv])

        for u, (d, h) in enumerate(units):
            o_ref = of_ref if d == 0 else ob_ref
            b = bs[d][:, h * dk:(h + 1) * dk]
            a2 = jnp.where(tbl_ref[d, T_PAIR + 3] > 0, rs[u][3], 0.0)
            for p in range(2, -1, -1):
                a2 = jnp.where(tbl_ref[d, T_PAIR + p] > 0, rs[u][p], a2)
            v2 = jnp.concatenate([vs[u], vs[u]], axis=0)
            o = _dot(a2.astype(bf16), v2) + _dot_nt((qss[u] * jnp.exp2(b)).astype(bf16),
                                                    s_ref[u].astype(bf16))
            o_ref[0, pl.ds(sts[d], L), h * dv:(h + 1) * dv] = o.astype(o_ref.dtype)

        for u, (d, h) in enumerate(units):
            b = bs[d][:, h * dk:(h + 1) * dk]
            bl = b[L - 1:L, :] if d == 0 else b[0:1, :]
            kt = (kfs[u] * jnp.exp2(bl - b)).astype(bf16)
            s_ref[u] = s_ref[u] * jnp.exp2(bl) + _dot_tn(vs[u], kt)
        return carry

    lax.fori_loop(0, nch, chunk, 0, unroll=True)


def _gla(qk, v, gd, gate_up, gate_b, tb=512):
    B, S, _ = qk.shape
    tb = min(tb, S)
    assert S % tb == 0 and tb % CHUNK == 0
    nb = S // tb
    nch = tb // CHUNK
    R = GLA_GATE_RANK
    gu = jnp.zeros((2, LANE, GLA_KW), f32)
    gu = gu.at[0, :R].set(gate_up[0]).at[1, R:2 * R].set(gate_up[1]).astype(bf16)
    gbias = gate_b.astype(f32).reshape(2, 1, GLA_KW)
    cum_np, tbl_np = _gla_tables()
    cum = jnp.asarray(cum_np, bf16)
    tbl = jnp.asarray(tbl_np, f32)

    def fwd(c):
        return lambda b, i: (b, i, c)

    def bwd(c):
        return lambda b, i: (b, nb - 1 - i, c)

    const3 = lambda b, i: (0, 0, 0)
    in_specs = []
    for mk in (fwd, bwd):
        in_specs += [pl.BlockSpec((1, tb, GLA_KW), mk(0)), pl.BlockSpec((1, tb, GLA_KW), mk(1)),
                     pl.BlockSpec((1, tb, GLA_VW), mk(0)), pl.BlockSpec((1, tb, LANE), mk(0))]
    in_specs += [pl.BlockSpec((2, LANE, GLA_KW), const3), pl.BlockSpec((2, 1, GLA_KW), const3),
                 pl.BlockSpec(cum.shape, const3), pl.BlockSpec(tbl.shape, lambda b, i: (0, 0, 0, 0))]
    return pl.pallas_call(
        functools.partial(_gla_body, nch=nch),
        grid=(B, nb),
        in_specs=in_specs,
        out_specs=[pl.BlockSpec((1, tb, GLA_VW), fwd(0)), pl.BlockSpec((1, tb, GLA_VW), bwd(0))],
        out_shape=[jax.ShapeDtypeStruct((B, S, GLA_VW), bf16)] * 2,
        scratch_shapes=[pltpu.VMEM((2 * GLA_HEADS, GLA_DV, GLA_DK), f32)],
        compiler_params=_cparams(("parallel", "arbitrary")),
        name="gla",
    )(qk, qk, v, gd, qk, qk, v, gd, gu, gbias, cum, tbl)


def _odd_out_body(x_ref, of_ref, ob_ref, r_ref, g_ref, w_ref, gp_ref, out_ref):
    o = of_ref[...].astype(f32) + ob_ref[...].astype(f32)
    hn = jnp.concatenate(
        [_rms(o[:, h * GLA_DV:(h + 1) * GLA_DV]) for h in range(GLA_HEADS)], axis=-1)
    r = r_ref[...].astype(f32)
    y = (hn * g_ref[...]) * (r * _sigmoid(r))
    m = _dot(y.astype(bf16), w_ref[...])
    out_ref[...] = x_ref[...] + _rms(m) * gp_ref[...]


def _odd_out(x2d, of2d, ob2d, r2d, norm_g, w_out, g_post, tm=512):
    T, Dm = x2d.shape
    row = lambda i: (i, 0)
    const = lambda i: (0, 0)
    return pl.pallas_call(
        _odd_out_body,
        grid=(T // tm,),
        in_specs=[pl.BlockSpec((tm, Dm), row), pl.BlockSpec((tm, GLA_VW), row),
                  pl.BlockSpec((tm, GLA_VW), row), pl.BlockSpec((tm, GLA_VW), row),
                  pl.BlockSpec((1, GLA_VW), const), pl.BlockSpec((GLA_VW, Dm), const),
                  pl.BlockSpec((1, Dm), const)],
        out_specs=pl.BlockSpec((tm, Dm), row),
        out_shape=jax.ShapeDtypeStruct((T, Dm), f32),
        compiler_params=_cparams(("parallel",)),
        name="odd_out",
    )(x2d, of2d, ob2d, r2d, norm_g.reshape(1, GLA_VW), w_out, g_post.reshape(1, Dm))


FFN_HALO = 8
FFN_COLS = 256
GELU_K2 = 2.0 * 0.7978845608028654
GELU_C = 0.044715


def _ffn_body(x_ref, xn_ref, xp_ref, gpre_ref, wu_ref, cw_ref, cb_ref, wd_ref, gpost_ref, p_ref,
              wg_ref, wp_ref, out_ref, lhs_ref, ue_ref, acc_ref, *, tm):
    i = pl.program_id(1)
    keep_prev = (i > 0).astype(f32)
    keep_next = (i < pl.num_programs(1) - 1).astype(f32)
    gpre = gpre_ref[...]
    x = x_ref[0]
    halo = jnp.concatenate([_rms(xn_ref[0]) * gpre * keep_next, _rms(xp_ref[0]) * gpre * keep_prev], axis=0)
    lhs_ref[...] = jnp.concatenate([(_rms(x) * gpre).astype(bf16), halo.astype(bf16)], axis=0)
    acc_ref[...] = jnp.zeros_like(acc_ref)
    te = tm + 2 * FFN_HALO
    nchunks = D_FF // FFN_COLS

    def cols(base, n):
        return pl.ds(pl.multiple_of(base + n * FFN_COLS, FFN_COLS), FFN_COLS)

    def up(n, slot):
        lhs = lhs_ref[...]
        ue_ref[slot, 0] = _dot(lhs, wu_ref[:, cols(D_FF, n)])
        ue_ref[slot, 1] = _dot(lhs, wu_ref[:, cols(0, n)])

    def conv(n, slot, half):
        ue = ue_ref[slot, half]
        cs = cols(D_FF * (1 - half), n)
        w = cw_ref[:, cs]
        above = pltpu.roll(ue, 1, axis=0)[:tm]
        below = pltpu.roll(ue, te - 1, axis=0)[:tm]
        return above * w[0:1] + ue[:tm] * w[1:2] + below * w[2:3] + cb_ref[:, cs]

    def down(n, slot):
        g = conv(n, slot, 0)
        e = jnp.exp2(g * (g * g * (-GELU_K2 * GELU_C * LOG2E) + (-GELU_K2 * LOG2E)))
        act = (g / (1.0 + e) * conv(n, slot, 1)).astype(bf16)
        acc_ref[...] += _dot(act, wd_ref[cols(0, n), :])

    up(0, 0)

    def pair(jj, carry):
        n = 2 * jj
        up(n + 1, 1)
        down(n, 0)
        up(n + 2, 0)
        down(n + 1, 1)
        return carry

    lax.fori_loop(0, (nchunks - 1) // 2, pair, 0)
    down(nchunks - 1, 0)
    x2 = x + _rms(acc_ref[...]) * gpost_ref[...]
    gate = _sigmoid(_dot(x2.astype(bf16), wg_ref[...]))
    out_ref[0] = x2 + _dot(p_ref[0, 0].astype(bf16), wp_ref[...]) * gate


def _resident(shape):
    nd = len(shape)
    return pl.BlockSpec(shape, lambda b, i: (0,) * nd, pipeline_mode=pl.Buffered(1))


def _ffn(x3, g_pre, w_up, conv_w, conv_b, w_down, g_post, p4, layer, w_gate, w_proj, tm=512):
    B, S, Dm = x3.shape
    NF = w_up.shape[1]
    PD = p4.shape[-1]
    tm = min(tm, S)
    nchunks = D_FF // FFN_COLS
    assert S % tm == 0 and tm % FFN_HALO == 0 and D_FF % FFN_COLS == 0 and NF == 2 * D_FF and nchunks % 2 == 1
    hb = tm // FFN_HALO
    nhalo = S // FFN_HALO
    te = tm + 2 * FFN_HALO
    blk = lambda b, i: (b, i, 0)
    return pl.pallas_call(
        functools.partial(_ffn_body, tm=tm),
        grid=(B, S // tm),
        in_specs=[pl.BlockSpec((1, tm, Dm), blk),
                  pl.BlockSpec((1, FFN_HALO, Dm), lambda b, i: (b, jnp.minimum((i + 1) * hb, nhalo - 1), 0)),
                  pl.BlockSpec((1, FFN_HALO, Dm), lambda b, i: (b, jnp.maximum(i * hb - 1, 0), 0)),
                  _resident((1, Dm)), _resident((Dm, NF)), _resident((3, NF)), _resident((1, NF)),
                  _resident((D_FF, Dm)), _resident((1, Dm)),
                  pl.BlockSpec((1, 1, tm, PD), lambda b, i: (layer, b, i, 0)),
                  _resident((Dm, Dm)), _resident((PD, Dm))],
        out_specs=pl.BlockSpec((1, tm, Dm), blk),
        out_shape=jax.ShapeDtypeStruct((B, S, Dm), f32),
        scratch_shapes=[pltpu.VMEM((te, Dm), bf16),
                        pltpu.VMEM((2, 2, te, FFN_COLS), f32),
                        pltpu.VMEM((tm, Dm), f32)],
        compiler_params=_cparams(("parallel", "parallel")),
        name="ffn",
    )(x3, x3, x3, g_pre.reshape(1, Dm), w_up, conv_w, conv_b.reshape(1, NF), w_down,
      g_post.reshape(1, Dm), p4, w_gate, w_proj)


def _pad_cols(w, n):
    return jnp.pad(w, ((0, 0), (0, n - w.shape[1])))


def _trunk(x, p, norm_mix_pre, norm_mix_post, norm_ffn_pre, norm_ffn_post,
           even_w_in, even_na_rpb, even_ml_gate_b, even_ml_norm_g, even_w_out,
           odd_w_in, odd_gate_up, odd_gate_b, odd_norm_g, odd_w_out,
           ffn_w_up, ffn_conv_w, ffn_conv_b, ffn_w_down, ple_w_proj, ple_w_gate):
    B, S, Dm = x.shape
    T = B * S
    depth = norm_mix_pre.shape[0]
    for i in range(depth):
        j = i // 2
        x2d = x.reshape(T, Dm)
        if i % 2 == 0:
            widths = (3 * NA_WIDTH, 4 * ML_WIDTH)
            w_in = even_w_in[j].astype(bf16)
            a3, m4, gt = _norm_matmul(x2d, norm_mix_pre[i], w_in[:, :sum(widths)], widths, (bf16, bf16),
                                      wt=w_in[:, sum(widths):].T)
            y_na = _neighborhood_attention(a3.reshape(B, S, -1), even_na_rpb[j])
            gt = jnp.transpose(gt.reshape(-1, B, S // CHUNK, CHUNK), (1, 2, 0, 3))
            hf, hb = _mlstm(m4.reshape(B, S, -1), gt, even_ml_gate_b[j])
            x2d = _even_out(x2d, y_na.reshape(T, -1), hf.reshape(T, -1), hb.reshape(T, -1), m4,
                            even_ml_norm_g[j], even_w_out[j].astype(bf16), norm_mix_post[i])
        else:
            widths = (2 * GLA_KW, GLA_VW, GLA_VW, LANE)
            w_in = _pad_cols(odd_w_in[j], sum(widths)).astype(bf16)
            qk, v, r, gd = _norm_matmul(x2d, norm_mix_pre[i], w_in, widths, (bf16, bf16, bf16, f32))
            of, ob = _gla(qk.reshape(B, S, -1), v.reshape(B, S, -1), gd.reshape(B, S, LANE),
                          odd_gate_up[j], odd_gate_b[j])
            x2d = _odd_out(x2d, of.reshape(T, -1), ob.reshape(T, -1), r, odd_norm_g[j],
                           odd_w_out[j].astype(bf16), norm_mix_post[i])
        x = _ffn(x2d.reshape(B, S, Dm), norm_ffn_pre[i], ffn_w_up[i].astype(bf16), ffn_conv_w[i],
                 ffn_conv_b[i], ffn_w_down[i].astype(bf16), norm_ffn_post[i], p, i,
                 ple_w_gate[i].astype(bf16), ple_w_proj[i].astype(bf16))
    return x


def kernel(x_prompt, x_sample, p_prompt, p_sample, norm_mix_pre, norm_mix_post, norm_ffn_pre, norm_ffn_post, even_w_in, even_na_rpb, even_ml_gate_b, even_ml_norm_g, even_w_out, odd_w_in, odd_gate_up, odd_gate_b, odd_norm_g, odd_w_out, ffn_w_up, ffn_conv_w, ffn_conv_b, ffn_w_down, ple_w_proj, ple_w_gate):
    params = (norm_mix_pre, norm_mix_post, norm_ffn_pre, norm_ffn_post,
              even_w_in, even_na_rpb, even_ml_gate_b, even_ml_norm_g, even_w_out,
              odd_w_in, odd_gate_up, odd_gate_b, odd_norm_g, odd_w_out,
              ffn_w_up, ffn_conv_w, ffn_conv_b, ffn_w_down, ple_w_proj, ple_w_gate)
    return (_trunk(x_prompt, p_prompt, *params), _trunk(x_sample, p_sample, *params))
```

```python
import functools

import jax
import jax.numpy as jnp
import numpy as np
from jax import lax
from jax.experimental import pallas as pl
from jax.experimental.pallas import tpu as pltpu

f32 = jnp.float32
bf16 = jnp.bfloat16

D_MODEL = 1024
EPS = 1e-6
GRID_W = 64
NA_HEADS = 8
NA_HEAD_DIM = 64
NA_WIN_ROWS = 8
NA_WIN_COLS = 16
NA_GROUP = 4
NA_WIDTH = NA_HEADS * NA_HEAD_DIM
ML_HEADS = 4
ML_HEAD_DIM = 128
ML_WIDTH = ML_HEADS * ML_HEAD_DIM
GLA_HEADS = 4
GLA_DK = 128
GLA_DV = 256
GLA_KW = GLA_HEADS * GLA_DK
GLA_VW = GLA_HEADS * GLA_DV
GLA_GATE_RANK = 16
GLA_TAU = 16.0
CHUNK = 64
D_FF = 2816
LANE = 128
NEG = -1e30
LOG2E = 1.4426950408889634

VMEM_LIMIT = 56 * 1024 * 1024


def _cparams(sem):
    return pltpu.CompilerParams(dimension_semantics=sem, vmem_limit_bytes=VMEM_LIMIT)


def _dot(a, b):
    return jnp.dot(a, b, preferred_element_type=f32)


def _dot_nt(a, b):
    return lax.dot_general(a, b, (((1,), (1,)), ((), ())), preferred_element_type=f32)


def _dot_tn(a, b):
    return lax.dot_general(a, b, (((0,), (0,)), ((), ())), preferred_element_type=f32)


def _dot_f32(a, b):
    return jnp.dot(a, b, precision=lax.Precision.HIGHEST, preferred_element_type=f32)


def _rms(x):
    return x * lax.rsqrt(jnp.mean(x * x, axis=-1, keepdims=True) + EPS)


def _sigmoid(x):
    return 1.0 / (1.0 + jnp.exp(-x))


def _log_sigmoid(x):
    return jnp.minimum(x, 0.0) - jnp.log(1.0 + jnp.exp(-jnp.abs(x)))


def _norm_matmul_body(x_ref, g_ref, w_ref, *refs, widths, with_t):
    hn = (_rms(x_ref[...]) * g_ref[...]).astype(bf16)
    o_refs = refs[1:] if with_t else refs
    off = 0
    for o_ref, n in zip(o_refs, widths):
        for c0 in range(0, n, 512):
            cw = min(512, n - c0)
            o_ref[:, c0:c0 + cw] = _dot(hn, w_ref[:, off + c0:off + c0 + cw]).astype(o_ref.dtype)
        off += n
    if with_t:
        o_refs[len(widths)][...] = _dot_nt(refs[0][...], hn)


def _norm_matmul(x2d, g, w, widths, dtypes, wt=None, tm=512):
    T, Dm = x2d.shape
    N = w.shape[1]
    assert sum(widths) == N and T % tm == 0
    const = lambda i: (0, 0)
    in_specs = [pl.BlockSpec((tm, Dm), lambda i: (i, 0)), pl.BlockSpec((1, Dm), const), pl.BlockSpec((Dm, N), const)]
    out_specs = [pl.BlockSpec((tm, n), lambda i: (i, 0)) for n in widths]
    out_shape = [jax.ShapeDtypeStruct((T, n), dt) for n, dt in zip(widths, dtypes)]
    args = [x2d, g.reshape(1, Dm), w]
    if wt is not None:
        in_specs.append(pl.BlockSpec(wt.shape, const))
        out_specs.append(pl.BlockSpec((wt.shape[0], tm), lambda i: (0, i)))
        out_shape.append(jax.ShapeDtypeStruct((wt.shape[0], T), f32))
        args.append(wt)
    return pl.pallas_call(
        functools.partial(_norm_matmul_body, widths=tuple(widths), with_t=wt is not None),
        grid=(T // tm,),
        in_specs=in_specs,
        out_specs=out_specs,
        out_shape=out_shape,
        compiler_params=_cparams(("parallel",)),
        name="norm_matmul",
    )(*args)


def _na_body(q_ref, k_ref, v_ref, bias_ref, o_ref, s_ref, *, R, rows):
    i = pl.program_id(2)
    lane = lax.broadcasted_iota(jnp.int32, (GRID_W, LANE), 1)
    first = lane < NA_HEAD_DIM
    nkeys = NA_WIN_ROWS * GRID_W
    ngroups = R // NA_GROUP

    def row_info(g, t):
        j = g * NA_GROUP + t
        r = i * R + j
        rs = jnp.clip(r - NA_WIN_ROWS // 2, 0, rows - NA_WIN_ROWS)
        return (r - rs, pl.multiple_of(j * GRID_W, GRID_W), pl.multiple_of(rs * GRID_W, GRID_W))

    def scores(g, slot):
        for t in range(NA_GROUP):
            case, qoff, kstart = row_info(g, t)
            q = q_ref[0, pl.ds(qoff, GRID_W), :]
            zero = jnp.zeros_like(q)
            q2 = jnp.concatenate([jnp.where(first, q, zero), jnp.where(first, zero, q)], axis=0)
            kw = k_ref[0, pl.ds(kstart, nkeys), :]
            s = _dot_nt(q2, kw) * (NA_HEAD_DIM ** -0.5 * LOG2E)
            s_ref[slot, t] = s + jnp.concatenate([bias_ref[case, 0], bias_ref[case, 1]], axis=0)

    def attend(g, slot):
        p_all, l_all = [], []
        for t in range(NA_GROUP):
            s = s_ref[slot, t]
            p = jnp.exp2(s - jnp.max(s, axis=-1, keepdims=True))
            l_all.append(jnp.sum(p, axis=-1, keepdims=True))
            p_all.append(p.astype(bf16))
        for t in range(NA_GROUP):
            _, qoff, kstart = row_info(g, t)
            vw = v_ref[0, pl.ds(kstart, nkeys), :]
            o = _dot(p_all[t], vw) / l_all[t]
            o_ref[0, pl.ds(qoff, GRID_W), :] = jnp.where(
                first, o[:GRID_W], o[GRID_W:]).astype(o_ref.dtype)

    scores(0, 0)

    def pair(jj, carry):
        g = 2 * jj
        scores(g + 1, 1)
        attend(g, 0)
        scores(g + 2, 0)
        attend(g + 1, 1)
        return carry

    lax.fori_loop(0, (ngroups - 2) // 2, pair, 0)
    scores(ngroups - 1, 1)
    attend(ngroups - 2, 0)
    attend(ngroups - 1, 1)


def _na_bias_table(rpb):
    nr, nc = 2 * NA_WIN_ROWS - 1, 2 * NA_WIN_COLS - 1
    cols = np.arange(GRID_W)
    cs = np.clip(cols - NA_WIN_COLS // 2, 0, GRID_W - NA_WIN_COLS)
    kc = np.arange(GRID_W)
    valid = (kc[None, :] >= cs[:, None]) & (kc[None, :] < cs[:, None] + NA_WIN_COLS)
    dc = kc[None, :] - cols[:, None] + (NA_WIN_COLS - 1)
    expand = (dc[None] == np.arange(nc)[:, None, None]).astype(np.float32).reshape(nc, -1)
    t = jnp.dot(rpb.astype(f32).reshape(-1, nc), expand, precision=lax.Precision.HIGHEST)
    t = t.reshape(NA_HEADS, nr, GRID_W, GRID_W)
    t = jnp.stack([t[:, NA_WIN_ROWS - 1 - off:2 * NA_WIN_ROWS - 1 - off] for off in range(NA_WIN_ROWS)])
    t = jnp.where(valid[None, None, None], t * LOG2E, NEG)
    t = jnp.transpose(t, (0, 1, 3, 2, 4))
    return t.reshape(NA_WIN_ROWS, NA_HEADS, GRID_W, NA_WIN_ROWS * GRID_W)


def _neighborhood_attention(a3, rpb, R=16):
    B, S, _ = a3.shape
    rows = S // GRID_W
    assert rows >= NA_WIN_ROWS and rows % R == 0 and R % (2 * NA_GROUP) == 0 and R >= 4 * NA_GROUP
    bias = _na_bias_table(rpb)
    npair = NA_WIDTH // LANE
    return pl.pallas_call(
        functools.partial(_na_body, R=R, rows=rows),
        grid=(B, npair, rows // R),
        in_specs=[pl.BlockSpec((1, R * GRID_W, LANE), lambda b, p, i: (b, i, p)),
                  pl.BlockSpec((1, S, LANE), lambda b, p, i: (b, 0, npair + p)),
                  pl.BlockSpec((1, S, LANE), lambda b, p, i: (b, 0, 2 * npair + p)),
                  pl.BlockSpec((NA_WIN_ROWS, 2, GRID_W, NA_WIN_ROWS * GRID_W), lambda b, p, i: (0, p, 0, 0))],
        out_specs=pl.BlockSpec((1, R * GRID_W, LANE), lambda b, p, i: (b, i, p)),
        out_shape=jax.ShapeDtypeStruct((B, S, NA_WIDTH), bf16),
        scratch_shapes=[pltpu.VMEM((2, NA_GROUP, 2 * GRID_W, NA_WIN_ROWS * GRID_W), f32)],
        compiler_params=_cparams(("parallel", "parallel", "arbitrary")),
        name="neighborhood_attention",
    )(a3, a3, a3, bias)


def _mlstm_tables():
    L = CHUNK
    t = np.arange(L)
    tril = (t[None, :] <= t[:, None]).astype(np.float32)
    masks = np.stack([tril, tril.T, np.eye(L, dtype=np.float32)])
    rhs = np.zeros((2 * L, 2 * LANE), np.float32)
    rhs[:L, :LANE] = 1.0
    rhs[:L, LANE:] = -1.0
    rhs[L:, LANE:] = 1.0
    return masks, rhs


def _mlstm_body(qf, kf, vf, qb, kb, vb, gtf, gtb, bcol, msk_ref, rhs_ref, hf_ref, hb_ref,
                c_ref, m_ref, *, nch):
    L = CHUNK
    H = ML_HEADS
    dh = ML_HEAD_DIM
    scale = dh ** -0.5
    units = [(d, h) for d in range(2) for h in range(H)]

    @pl.when(pl.program_id(1) == 0)
    def _():
        c_ref[...] = jnp.zeros_like(c_ref)
        m_ref[...] = jnp.zeros_like(m_ref)

    sub = lax.broadcasted_iota(jnp.int32, (4 * H, 1), 0)
    f_row = (sub // H) % 2 == 1

    def chunk(j, carry):
        cjs = (j, nch - 1 - j)
        sts = [pl.multiple_of(cj * L, L) for cj in cjs]
        tril, triu, eye = msk_ref[0], msk_ref[1], msk_ref[2]
        rhs = rhs_ref[...]
        gts, brows = [], []
        for d, gt_ref in enumerate((gtf, gtb)):
            gt = gt_ref[0, cjs[d]] + bcol[...]
            gt = jnp.where(f_row, _log_sigmoid(gt), gt) * LOG2E
            gts.append(gt)
            brows.append(_dot_f32(gt, triu if d == 0 else tril))

        bcols, wcols = [], []
        for d, h in units:
            li_row = gts[d][d * 2 * H + h:d * 2 * H + h + 1, :]
            lf_row = gts[d][d * 2 * H + H + h:d * 2 * H + H + h + 1, :]
            lhs = jnp.concatenate([(tril if d == 0 else triu) * lf_row, eye * li_row], axis=1)
            hi = lhs.astype(bf16)
            lo = (lhs - hi.astype(f32)).astype(bf16)
            p = _dot(hi, rhs) + _dot(lo, rhs)
            bcols.append(p[:, :LANE])
            wcols.append(p[:, LANE:])

        qs, ks, vs, qks = [], [], [], []
        for d, h in units:
            q_ref, k_ref, v_ref = (qf, kf, vf) if d == 0 else (qb, kb, vb)
            hs = slice(h * dh, (h + 1) * dh)
            qs.append(q_ref[0, pl.ds(sts[d], L), hs])
            ks.append(k_ref[0, pl.ds(sts[d], L), hs])
            vs.append(v_ref[0, pl.ds(sts[d], L), hs])
            qks.append(_dot_nt(qs[-1], ks[-1]))

        scs, aiss, mts = [], [], []
        for u, (d, h) in enumerate(units):
            li_row = gts[d][d * 2 * H + h:d * 2 * H + h + 1, :]
            b_row = brows[d][d * 2 * H + H + h:d * 2 * H + H + h + 1, :]
            m_prev = m_ref[u]
            dm = jnp.where((tril if d == 0 else triu) > 0, bcols[u][:, :L] - b_row + li_row, NEG)
            inter = bcols[u] + m_prev
            mt = jnp.maximum(inter, jnp.max(dm, axis=-1, keepdims=True))
            scs.append(qks[u] * scale * jnp.exp2(dm - mt[:, :L]))
            aiss.append(jnp.exp2(inter - mt) * scale)
            mts.append(mt)

        nvs, qcs = [], []
        for u in range(len(units)):
            nvs.append(_dot(scs[u].astype(bf16), vs[u]))
            qcs.append(_dot(qs[u], c_ref[u].astype(bf16)))

        for u, (d, h) in enumerate(units):
            h_ref = hf_ref if d == 0 else hb_ref
            num = nvs[u] + aiss[u] * qcs[u][:, :dh]
            den = jnp.sum(scs[u], axis=-1, keepdims=True) + aiss[u] * qcs[u][:, dh:]
            h_ref[0, pl.ds(sts[d], L), h * dh:(h + 1) * dh] = (
                num / jnp.maximum(jnp.abs(den), jnp.exp2(-mts[u]))).astype(h_ref.dtype)

        for u, (d, h) in enumerate(units):
            m_prev = m_ref[u]
            bl = bcols[u][L - 1:L, :] if d == 0 else bcols[u][0:1, :]
            ds = wcols[u] + bl
            m_new = jnp.maximum(bl + m_prev, jnp.max(ds, axis=0, keepdims=True))
            decay = jnp.exp2(bl + m_prev - m_new)
            ws = jnp.exp2(ds - m_new)
            wv = jnp.concatenate([ws * vs[u].astype(f32), ws], axis=1).astype(bf16)
            c_ref[u] = jnp.concatenate([decay, decay], axis=1) * c_ref[u] + _dot_tn(ks[u], wv)
            m_ref[u] = m_new
        return carry

    lax.fori_loop(0, nch, chunk, 0, unroll=True)


def _mlstm(m4, gt, gate_b, tb=512):
    B, S, _ = m4.shape
    tb = min(tb, S)
    assert S % tb == 0 and tb % CHUNK == 0
    nb = S // tb
    nch = tb // CHUNK
    ng = 4 * ML_HEADS
    bcol = gate_b.astype(f32).reshape(ng, 1)
    masks_np, rhs_np = _mlstm_tables()
    masks = jnp.asarray(masks_np, f32)
    rhs = jnp.asarray(rhs_np, bf16)
    W = ML_WIDTH

    def fwd(c):
        return lambda b, i: (b, i, c)

    def bwd(c):
        return lambda b, i: (b, nb - 1 - i, c)

    in_specs = (
        [pl.BlockSpec((1, tb, W), fwd(c)) for c in range(3)]
        + [pl.BlockSpec((1, tb, W), bwd(c)) for c in range(3)]
        + [pl.BlockSpec((1, nch, ng, CHUNK), lambda b, i: (b, i, 0, 0)),
           pl.BlockSpec((1, nch, ng, CHUNK), lambda b, i: (b, nb - 1 - i, 0, 0)),
           pl.BlockSpec((ng, 1), lambda b, i: (0, 0)),
           pl.BlockSpec(masks.shape, lambda b, i: (0, 0, 0)),
           pl.BlockSpec(rhs.shape, lambda b, i: (0, 0))])
    return pl.pallas_call(
        functools.partial(_mlstm_body, nch=nch),
        grid=(B, nb),
        in_specs=in_specs,
        out_specs=[pl.BlockSpec((1, tb, W), fwd(0)), pl.BlockSpec((1, tb, W), bwd(0))],
        out_shape=[jax.ShapeDtypeStruct((B, S, W), bf16)] * 2,
        scratch_shapes=[pltpu.VMEM((2 * ML_HEADS, ML_HEAD_DIM, 2 * ML_HEAD_DIM), f32),
                        pltpu.VMEM((2 * ML_HEADS, 1, LANE), f32)],
        compiler_params=_cparams(("parallel", "arbitrary")),
        name="mlstm",
    )(m4, m4, m4, m4, m4, m4, gt, gt, bcol, masks, rhs)


def _even_out_body(x_ref, na_ref, hf_ref, hb_ref, o_ref, g_ref, w_ref, gp_ref, out_ref):
    hm = hf_ref[...].astype(f32) + hb_ref[...].astype(f32)
    hn = jnp.concatenate(
        [_rms(hm[:, h * ML_HEAD_DIM:(h + 1) * ML_HEAD_DIM]) for h in range(ML_HEADS)], axis=-1)
    y_ml = _sigmoid(o_ref[...].astype(f32)) * (hn * g_ref[...])
    m = _dot(na_ref[...], w_ref[:NA_WIDTH, :]) + _dot(y_ml.astype(bf16), w_ref[NA_WIDTH:, :])
    out_ref[...] = x_ref[...] + _rms(m) * gp_ref[...]


def _even_out(x2d, na2d, hf2d, hb2d, m4_2d, ml_g, w_out, g_post, tm=1024):
    T, Dm = x2d.shape
    W = ML_WIDTH
    row = lambda i: (i, 0)
    const = lambda i: (0, 0)
    return pl.pallas_call(
        _even_out_body,
        grid=(T // tm,),
        in_specs=[pl.BlockSpec((tm, Dm), row), pl.BlockSpec((tm, NA_WIDTH), row),
                  pl.BlockSpec((tm, W), row), pl.BlockSpec((tm, W), row),
                  pl.BlockSpec((tm, W), lambda i: (i, 3)),
                  pl.BlockSpec((1, W), const), pl.BlockSpec((NA_WIDTH + W, Dm), const),
                  pl.BlockSpec((1, Dm), const)],
        out_specs=pl.BlockSpec((tm, Dm), row),
        out_shape=jax.ShapeDtypeStruct((T, Dm), f32),
        compiler_params=_cparams(("parallel",)),
        name="even_out",
    )(x2d, na2d, hf2d, hb2d, m4_2d, ml_g.reshape(1, W), w_out, g_post.reshape(1, Dm))


GLA_BCAST_LEVELS = (32, 16, 8, 4)
T_SGN, T_COEF, T_PAIR = 0, 4, 8


def _gla_tables():
    L = CHUNK
    t = np.arange(L)
    tril = (t[None, :] <= t[:, None]).astype(np.float32)
    cum = np.stack([tril, tril.T])
    tbl = np.zeros((2, 12, L, LANE), np.float32)
    pm = np.zeros((2, 7, L, L), np.float32)
    for lv, g in enumerate((32, 16, 8, 4, 2, 1)):
        a = (t // (2 * g)) * 2 * g
        upper = (t - a) >= g
        same = a[:, None] == a[None, :]
        pm[0, lv] = same & upper[:, None] & (~upper)[None, :]
        pm[1, lv] = same & (~upper)[:, None] & upper[None, :]
        if g >= 4:
            tbl[0, T_SGN + lv] = np.where(upper, 1.0, -1.0)[:, None]
            tbl[1, T_SGN + lv] = np.where(upper, -1.0, 1.0)[:, None]
    pm[:, 6] = np.eye(L)
    r4, r2 = t % 4, t % 2
    tbl[0, T_COEF + 0] = np.isin(r4, (2, 3))[:, None]
    tbl[0, T_COEF + 1] = (r4 == 3)[:, None]
    tbl[0, T_COEF + 2] = (r4 == 0)[:, None]
    tbl[0, T_COEF + 3] = (r2 == 1)[:, None]
    tbl[1, T_COEF + 0] = np.isin(r4, (0, 1))[:, None]
    tbl[1, T_COEF + 1] = (r4 == 3)[:, None]
    tbl[1, T_COEF + 2] = (r4 == 0)[:, None]
    tbl[1, T_COEF + 3] = (r2 == 0)[:, None]
    for p in range(4):
        tbl[:, T_PAIR + p, :, :L] = pm[:, 2 * p]
        if p < 3:
            tbl[:, T_PAIR + p, :, L:] = pm[:, 2 * p + 1]
    return cum, tbl


def _gla_body(qf, kf, vf, gdf, qb, kb, vb, gdb, gu_ref, gbias_ref, cum_ref, tbl_ref,
              of_ref, ob_ref, s_ref, *, nch):
    L = CHUNK
    H = GLA_HEADS
    dk = GLA_DK
    dv = GLA_DV
    scale = dk ** -0.5
    units = [(d, h) for d in range(2) for h in range(H)]

    @pl.when(pl.program_id(1) == 0)
    def _():
        s_ref[...] = jnp.zeros_like(s_ref)

    def chunk(j, carry):
        cjs = (j, nch - 1 - j)
        sts = [pl.multiple_of(cj * L, L) for cj in cjs]
        zk = jnp.zeros((L, dk), bf16)

        las, bs = [], []
        for d, gd_ref in enumerate((gdf, gdb)):
            gd = gd_ref[0, pl.ds(sts[d], L), :].astype(bf16)
            la = _log_sigmoid(_dot(gd, gu_ref[d]) + gbias_ref[d]) * (LOG2E / GLA_TAU)
            hi = la.astype(bf16)
            lo = (la - hi.astype(f32)).astype(bf16)
            las.append(la)
            bs.append(_dot(cum_ref[d], hi) + _dot(cum_ref[d], lo))

        qss, kfs, vs, rs = [], [], [], []
        for d, h in units:
            q_ref, k_ref, v_ref = (qf, kf, vf) if d == 0 else (qb, kb, vb)
            ks = slice(h * dk, (h + 1) * dk)
            la = las[d][:, ks]
            b = bs[d][:, ks]
            k = k_ref[0, pl.ds(sts[d], L), ks]
            qs = q_ref[0, pl.ds(sts[d], L), ks].astype(f32) * scale
            kf32 = k.astype(f32)
            xs = []
            for lv, g in enumerate(GLA_BCAST_LEVELS):
                refs = [a + g - 1 + d for a in range(0, L, 2 * g)]
                bref = jnp.concatenate(
                    [jnp.broadcast_to(b[r:r + 1, :], (2 * g, dk)) for r in refs], axis=0)
                xs.append(jnp.exp2((b - bref) * tbl_ref[d, T_SGN + lv]))
            la_dn = pltpu.roll(la, 1, axis=0)
            la_up = pltpu.roll(la, L - 1, axis=0)
            xs.append(jnp.exp2(la * tbl_ref[d, T_COEF] + la_dn * tbl_ref[d, T_COEF + 1]
                               + la_up * tbl_ref[d, T_COEF + 2]))
            xs.append(jnp.exp2(la * tbl_ref[d, T_COEF + 3]))
            r_u = []
            for p in range(3):
                xa, xb = xs[2 * p], xs[2 * p + 1]
                lhs = jnp.concatenate([(qs * xa).astype(bf16), (qs * xb).astype(bf16)], axis=1)
                rhs = jnp.concatenate(
                    [jnp.concatenate([(kf32 * xa).astype(bf16), zk], axis=1),
                     jnp.concatenate([zk, (kf32 * xb).astype(bf16)], axis=1)], axis=0)
                r_u.append(_dot_nt(lhs, rhs))
            r_u.append(_dot_nt(qs.astype(bf16), jnp.concatenate([k, zk], axis=0)))
            rs.append(r_u)
            qss.append(qs)
            kfs.append(kf32)
            vs.append(v_ref[0, pl.ds(sts[d], L), h * dv:(h + 1) * dv])

        for u, (d, h) in enumerate(units):
            o_ref = of_ref if d == 0 else ob_ref
            b = bs[d][:, h * dk:(h + 1) * dk]
            a2 = jnp.where(tbl_ref[d, T_PAIR + 3] > 0, rs[u][3], 0.0)
            for p in range(2, -1, -1):
                a2 = jnp.where(tbl_ref[d, T_PAIR + p] > 0, rs[u][p], a2)
            v2 = jnp.concatenate([vs[u], vs[u]], axis=0)
            o = _dot(a2.astype(bf16), v2) + _dot_nt((qss[u] * jnp.exp2(b)).astype(bf16),
                                                    s_ref[u].astype(bf16))
            o_ref[0, pl.ds(sts[d], L), h * dv:(h + 1) * dv] = o.astype(o_ref.dtype)

        for u, (d, h) in enumerate(units):
            b = bs[d][:, h * dk:(h + 1) * dk]
            bl = b[L - 1:L, :] if d == 0 else b[0:1, :]
            kt = (kfs[u] * jnp.exp2(bl - b)).astype(bf16)
            s_ref[u] = s_ref[u] * jnp.exp2(bl) + _dot_tn(vs[u], kt)
        return carry

    lax.fori_loop(0, nch, chunk, 0, unroll=True)


def _gla(qk, v, gd, gate_up, gate_b, tb=512):
    B, S, _ = qk.shape
    tb = min(tb, S)
    assert S % tb == 0 and tb % CHUNK == 0
    nb = S // tb
    nch = tb // CHUNK
    R = GLA_GATE_RANK
    gu = jnp.zeros((2, LANE, GLA_KW), f32)
    gu = gu.at[0, :R].set(gate_up[0]).at[1, R:2 * R].set(gate_up[1]).astype(bf16)
    gbias = gate_b.astype(f32).reshape(2, 1, GLA_KW)
    cum_np, tbl_np = _gla_tables()
    cum = jnp.asarray(cum_np, bf16)
    tbl = jnp.asarray(tbl_np, f32)

    def fwd(c):
        return lambda b, i: (b, i, c)

    def bwd(c):
        return lambda b, i: (b, nb - 1 - i, c)

    const3 = lambda b, i: (0, 0, 0)
    in_specs = []
    for mk in (fwd, bwd):
        in_specs += [pl.BlockSpec((1, tb, GLA_KW), mk(0)), pl.BlockSpec((1, tb, GLA_KW), mk(1)),
                     pl.BlockSpec((1, tb, GLA_VW), mk(0)), pl.BlockSpec((1, tb, LANE), mk(0))]
    in_specs += [pl.BlockSpec((2, LANE, GLA_KW), const3), pl.BlockSpec((2, 1, GLA_KW), const3),
                 pl.BlockSpec(cum.shape, const3), pl.BlockSpec(tbl.shape, lambda b, i: (0, 0, 0, 0))]
    return pl.pallas_call(
        functools.partial(_gla_body, nch=nch),
        grid=(B, nb),
        in_specs=in_specs,
        out_specs=[pl.BlockSpec((1, tb, GLA_VW), fwd(0)), pl.BlockSpec((1, tb, GLA_VW), bwd(0))],
        out_shape=[jax.ShapeDtypeStruct((B, S, GLA_VW), bf16)] * 2,
        scratch_shapes=[pltpu.VMEM((2 * GLA_HEADS, GLA_DV, GLA_DK), f32)],
        compiler_params=_cparams(("parallel", "arbitrary")),
        name="gla",
    )(qk, qk, v, gd, qk, qk, v, gd, gu, gbias, cum, tbl)


def _odd_out_body(x_ref, of_ref, ob_ref, r_ref, g_ref, w_ref, gp_ref, out_ref):
    o = of_ref[...].astype(f32) + ob_ref[...].astype(f32)
    hn = jnp.concatenate(
        [_rms(o[:, h * GLA_DV:(h + 1) * GLA_DV]) for h in range(GLA_HEADS)], axis=-1)
    r = r_ref[...].astype(f32)
    y = (hn * g_ref[...]) * (r * _sigmoid(r))
    m = _dot(y.astype(bf16), w_ref[...])
    out_ref[...] = x_ref[...] + _rms(m) * gp_ref[...]


def _odd_out(x2d, of2d, ob2d, r2d, norm_g, w_out, g_post, tm=1024):
    T, Dm = x2d.shape
    row = lambda i: (i, 0)
    const = lambda i: (0, 0)
    return pl.pallas_call(
        _odd_out_body,
        grid=(T // tm,),
        in_specs=[pl.BlockSpec((tm, Dm), row), pl.BlockSpec((tm, GLA_VW), row),
                  pl.BlockSpec((tm, GLA_VW), row), pl.BlockSpec((tm, GLA_VW), row),
                  pl.BlockSpec((1, GLA_VW), const), pl.BlockSpec((GLA_VW, Dm), const),
                  pl.BlockSpec((1, Dm), const)],
        out_specs=pl.BlockSpec((tm, Dm), row),
        out_shape=jax.ShapeDtypeStruct((T, Dm), f32),
        compiler_params=_cparams(("parallel",)),
        name="odd_out",
    )(x2d, of2d, ob2d, r2d, norm_g.reshape(1, GLA_VW), w_out, g_post.reshape(1, Dm))


FFN_HALO = 8
FFN_COLS = 256
GELU_K2 = 2.0 * 0.7978845608028654
GELU_C = 0.044715


def _ffn_body(x_ref, xn_ref, xp_ref, gpre_ref, wu_ref, cw_ref, cb_ref, wd_ref, gpost_ref, p_ref,
              wg_ref, wp_ref, out_ref, lhs_ref, ue_ref, acc_ref, *, tm):
    i = pl.program_id(1)
    keep_prev = (i > 0).astype(f32)
    keep_next = (i < pl.num_programs(1) - 1).astype(f32)
    gpre = gpre_ref[...]
    x = x_ref[0]
    halo = jnp.concatenate([_rms(xn_ref[0]) * gpre * keep_next, _rms(xp_ref[0]) * gpre * keep_prev], axis=0)
    lhs_ref[...] = jnp.concatenate([(_rms(x) * gpre).astype(bf16), halo.astype(bf16)], axis=0)
    acc_ref[...] = jnp.zeros_like(acc_ref)
    te = tm + 2 * FFN_HALO
    nchunks = D_FF // FFN_COLS

    def cols(base, n):
        return pl.ds(pl.multiple_of(base + n * FFN_COLS, FFN_COLS), FFN_COLS)

    def up(n, slot):
        lhs = lhs_ref[...]
        ue_ref[slot, 0] = _dot(lhs, wu_ref[:, cols(D_FF, n)])
        ue_ref[slot, 1] = _dot(lhs, wu_ref[:, cols(0, n)])

    def conv(n, slot, half):
        ue = ue_ref[slot, half]
        cs = cols(D_FF * (1 - half), n)
        w = cw_ref[:, cs]
        above = pltpu.roll(ue, 1, axis=0)[:tm]
        below = pltpu.roll(ue, te - 1, axis=0)[:tm]
        return above * w[0:1] + ue[:tm] * w[1:2] + below * w[2:3] + cb_ref[:, cs]

    def down(n, slot):
        g = conv(n, slot, 0)
        e = jnp.exp2(g * (g * g * (-GELU_K2 * GELU_C * LOG2E) + (-GELU_K2 * LOG2E)))
        act = (g / (1.0 + e) * conv(n, slot, 1)).astype(bf16)
        acc_ref[...] += _dot(act, wd_ref[cols(0, n), :])

    up(0, 0)

    def pair(jj, carry):
        n = 2 * jj
        up(n + 1, 1)
        down(n, 0)
        up(n + 2, 0)
        down(n + 1, 1)
        return carry

    lax.fori_loop(0, (nchunks - 1) // 2, pair, 0)
    down(nchunks - 1, 0)
    x2 = x + _rms(acc_ref[...]) * gpost_ref[...]
    gate = _sigmoid(_dot(x2.astype(bf16), wg_ref[...]))
    out_ref[0] = x2 + _dot(p_ref[0, 0].astype(bf16), wp_ref[...]) * gate


def _resident(shape):
    nd = len(shape)
    return pl.BlockSpec(shape, lambda b, i: (0,) * nd, pipeline_mode=pl.Buffered(1))


def _ffn(x3, g_pre, w_up, conv_w, conv_b, w_down, g_post, p4, layer, w_gate, w_proj, tm=512):
    B, S, Dm = x3.shape
    NF = w_up.shape[1]
    PD = p4.shape[-1]
    tm = min(tm, S)
    nchunks = D_FF // FFN_COLS
    assert S % tm == 0 and tm % FFN_HALO == 0 and D_FF % FFN_COLS == 0 and NF == 2 * D_FF and nchunks % 2 == 1
    hb = tm // FFN_HALO
    nhalo = S // FFN_HALO
    te = tm + 2 * FFN_HALO
    blk = lambda b, i: (b, i, 0)
    return pl.pallas_call(
        functools.partial(_ffn_body, tm=tm),
        grid=(B, S // tm),
        in_specs=[pl.BlockSpec((1, tm, Dm), blk),
                  pl.BlockSpec((1, FFN_HALO, Dm), lambda b, i: (b, jnp.minimum((i + 1) * hb, nhalo - 1), 0)),
                  pl.BlockSpec((1, FFN_HALO, Dm), lambda b, i: (b, jnp.maximum(i * hb - 1, 0), 0)),
                  _resident((1, Dm)), _resident((Dm, NF)), _resident((3, NF)), _resident((1, NF)),
                  _resident((D_FF, Dm)), _resident((1, Dm)),
                  pl.BlockSpec((1, 1, tm, PD), lambda b, i: (layer, b, i, 0)),
                  _resident((Dm, Dm)), _resident((PD, Dm))],
        out_specs=pl.BlockSpec((1, tm, Dm), blk),
        out_shape=jax.ShapeDtypeStruct((B, S, Dm), f32),
        scratch_shapes=[pltpu.VMEM((te, Dm), bf16),
                        pltpu.VMEM((2, 2, te, FFN_COLS), f32),
                        pltpu.VMEM((tm, Dm), f32)],
        compiler_params=_cparams(("parallel", "parallel")),
        name="ffn",
    )(x3, x3, x3, g_pre.reshape(1, Dm), w_up, conv_w, conv_b.reshape(1, NF), w_down,
      g_post.reshape(1, Dm), p4, w_gate, w_proj)


def _pad_cols(w, n):
    return jnp.pad(w, ((0, 0), (0, n - w.shape[1])))


def _trunk(x, p, norm_mix_pre, norm_mix_post, norm_ffn_pre, norm_ffn_post,
           even_w_in, even_na_rpb, even_ml_gate_b, even_ml_norm_g, even_w_out,
           odd_w_in, odd_gate_up, odd_gate_b, odd_norm_g, odd_w_out,
           ffn_w_up, ffn_conv_w, ffn_conv_b, ffn_w_down, ple_w_proj, ple_w_gate):
    B, S, Dm = x.shape
    T = B * S
    depth = norm_mix_pre.shape[0]
    for i in range(depth):
        j = i // 2
        x2d = x.reshape(T, Dm)
        if i % 2 == 0:
            widths = (3 * NA_WIDTH, 4 * ML_WIDTH)
            w_in = even_w_in[j].astype(bf16)
            a3, m4, gt = _norm_matmul(x2d, norm_mix_pre[i], w_in[:, :sum(widths)], widths, (bf16, bf16),
                                      wt=w_in[:, sum(widths):].T)
            y_na = _neighborhood_attention(a3.reshape(B, S, -1), even_na_rpb[j])
            gt = jnp.transpose(gt.reshape(-1, B, S // CHUNK, CHUNK), (1, 2, 0, 3))
            hf, hb = _mlstm(m4.reshape(B, S, -1), gt, even_ml_gate_b[j])
            x2d = _even_out(x2d, y_na.reshape(T, -1), hf.reshape(T, -1), hb.reshape(T, -1), m4,
                            even_ml_norm_g[j], even_w_out[j].astype(bf16), norm_mix_post[i])
        else:
            widths = (2 * GLA_KW, GLA_VW, GLA_VW, LANE)
            w_in = _pad_cols(odd_w_in[j], sum(widths)).astype(bf16)
            qk, v, r, gd = _norm_matmul(x2d, norm_mix_pre[i], w_in, widths, (bf16, bf16, bf16, f32))
            of, ob = _gla(qk.reshape(B, S, -1), v.reshape(B, S, -1), gd.reshape(B, S, LANE),
                          odd_gate_up[j], odd_gate_b[j])
            x2d = _odd_out(x2d, of.reshape(T, -1), ob.reshape(T, -1), r, odd_norm_g[j],
                           odd_w_out[j].astype(bf16), norm_mix_post[i])
        x = _ffn(x2d.reshape(B, S, Dm), norm_ffn_pre[i], ffn_w_up[i].astype(bf16), ffn_conv_w[i],
                 ffn_conv_b[i], ffn_w_down[i].astype(bf16), norm_ffn_post[i], p, i,
                 ple_w_gate[i].astype(bf16), ple_w_proj[i].astype(bf16))
    return x


def kernel(x_prompt, x_sample, p_prompt, p_sample, norm_mix_pre, norm_mix_post, norm_ffn_pre, norm_ffn_post, even_w_in, even_na_rpb, even_ml_gate_b, even_ml_norm_g, even_w_out, odd_w_in, odd_gate_up, odd_gate_b, odd_norm_g, odd_w_out, ffn_w_up, ffn_conv_w, ffn_conv_b, ffn_w_down, ple_w_proj, ple_w_gate):
    params = (norm_mix_pre, norm_mix_post, norm_ffn_pre, norm_ffn_post,
              even_w_in, even_na_rpb, even_ml_gate_b, even_ml_norm_g, even_w_out,
              odd_w_in, odd_gate_up, odd_gate_b, odd_norm_g, odd_w_out,
              ffn_w_up, ffn_conv_w, ffn_conv_b, ffn_w_down, ple_w_proj, ple_w_gate)
    return (_trunk(x_prompt, p_prompt, *params), _trunk(x_sample, p_sample, *params))
```

```python
import functools

import jax
import jax.numpy as jnp
import numpy as np
from jax import lax
from jax.experimental import pallas as pl
from jax.experimental.pallas import tpu as pltpu

f32 = jnp.float32
bf16 = jnp.bfloat16

D_MODEL = 1024
EPS = 1e-6
GRID_W = 64
NA_HEADS = 8
NA_HEAD_DIM = 64
NA_WIN_ROWS = 8
NA_WIN_COLS = 16
NA_GROUP = 4
NA_WIDTH = NA_HEADS * NA_HEAD_DIM
ML_HEADS = 4
ML_HEAD_DIM = 128
ML_WIDTH = ML_HEADS * ML_HEAD_DIM
GLA_HEADS = 4
GLA_DK = 128
GLA_DV = 256
GLA_KW = GLA_HEADS * GLA_DK
GLA_VW = GLA_HEADS * GLA_DV
GLA_GATE_RANK = 16
GLA_TAU = 16.0
CHUNK = 64
D_FF = 2816
LANE = 128
NEG = -1e30
LOG2E = 1.4426950408889634

VMEM_LIMIT = 56 * 1024 * 1024


def _cparams(sem):
    return pltpu.CompilerParams(dimension_semantics=sem, vmem_limit_bytes=VMEM_LIMIT)


def _dot(a, b):
    return jnp.dot(a, b, preferred_element_type=f32)


def _dot_nt(a, b):
    return lax.dot_general(a, b, (((1,), (1,)), ((), ())), preferred_element_type=f32)


def _dot_tn(a, b):
    return lax.dot_general(a, b, (((0,), (0,)), ((), ())), preferred_element_type=f32)


def _dot_f32(a, b):
    return jnp.dot(a, b, precision=lax.Precision.HIGHEST, preferred_element_type=f32)


def _rms(x):
    return x * lax.rsqrt(jnp.mean(x * x, axis=-1, keepdims=True) + EPS)


def _sigmoid(x):
    return 1.0 / (1.0 + jnp.exp(-x))


def _log_sigmoid(x):
    return jnp.minimum(x, 0.0) - jnp.log(1.0 + jnp.exp(-jnp.abs(x)))


def _norm_matmul_body(x_ref, g_ref, w_ref, *refs, widths, with_t):
    hn = (_rms(x_ref[...]) * g_ref[...]).astype(bf16)
    o_refs = refs[1:] if with_t else refs
    off = 0
    for o_ref, n in zip(o_refs, widths):
        for c0 in range(0, n, 512):
            cw = min(512, n - c0)
            o_ref[:, c0:c0 + cw] = _dot(hn, w_ref[:, off + c0:off + c0 + cw]).astype(o_ref.dtype)
        off += n
    if with_t:
        o_refs[len(widths)][...] = _dot_nt(refs[0][...], hn)


def _norm_matmul(x2d, g, w, widths, dtypes, wt=None, tm=512):
    T, Dm = x2d.shape
    N = w.shape[1]
    assert sum(widths) == N and T % tm == 0
    const = lambda i: (0, 0)
    in_specs = [pl.BlockSpec((tm, Dm), lambda i: (i, 0)), pl.BlockSpec((1, Dm), const), pl.BlockSpec((Dm, N), const)]
    out_specs = [pl.BlockSpec((tm, n), lambda i: (i, 0)) for n in widths]
    out_shape = [jax.ShapeDtypeStruct((T, n), dt) for n, dt in zip(widths, dtypes)]
    args = [x2d, g.reshape(1, Dm), w]
    if wt is not None:
        in_specs.append(pl.BlockSpec(wt.shape, const))
        out_specs.append(pl.BlockSpec((wt.shape[0], tm), lambda i: (0, i)))
        out_shape.append(jax.ShapeDtypeStruct((wt.shape[0], T), f32))
        args.append(wt)
    return pl.pallas_call(
        functools.partial(_norm_matmul_body, widths=tuple(widths), with_t=wt is not None),
        grid=(T // tm,),
        in_specs=in_specs,
        out_specs=out_specs,
        out_shape=out_shape,
        compiler_params=_cparams(("parallel",)),
        name="norm_matmul",
    )(*args)


def _na_body(q_ref, k_ref, v_ref, bias_ref, o_ref, s_ref, *, R, rows):
    i = pl.program_id(2)
    lane = lax.broadcasted_iota(jnp.int32, (GRID_W, LANE), 1)
    first = lane < NA_HEAD_DIM
    nkeys = NA_WIN_ROWS * GRID_W
    ngroups = R // NA_GROUP

    def row_info(g, t):
        j = g * NA_GROUP + t
        r = i * R + j
        rs = jnp.clip(r - NA_WIN_ROWS // 2, 0, rows - NA_WIN_ROWS)
        return (r - rs, pl.multiple_of(j * GRID_W, GRID_W), pl.multiple_of(rs * GRID_W, GRID_W))

    def scores(g, slot):
        for t in range(NA_GROUP):
            case, qoff, kstart = row_info(g, t)
            q = q_ref[0, pl.ds(qoff, GRID_W), :]
            zero = jnp.zeros_like(q)
            q2 = jnp.concatenate([jnp.where(first, q, zero), jnp.where(first, zero, q)], axis=0)
            kw = k_ref[0, pl.ds(kstart, nkeys), :]
            s = _dot_nt(q2, kw) * (NA_HEAD_DIM ** -0.5 * LOG2E)
            s_ref[slot, t] = s + jnp.concatenate([bias_ref[case, 0], bias_ref[case, 1]], axis=0)

    def attend(g, slot):
        p_all, l_all = [], []
        for t in range(NA_GROUP):
            s = s_ref[slot, t]
            p = jnp.exp2(s - jnp.max(s, axis=-1, keepdims=True))
            l_all.append(jnp.sum(p, axis=-1, keepdims=True))
            p_all.append(p.astype(bf16))
        for t in range(NA_GROUP):
            _, qoff, kstart = row_info(g, t)
            vw = v_ref[0, pl.ds(kstart, nkeys), :]
            o = _dot(p_all[t], vw) / l_all[t]
            o_ref[0, pl.ds(qoff, GRID_W), :] = jnp.where(
                first, o[:GRID_W], o[GRID_W:]).astype(o_ref.dtype)

    scores(0, 0)

    for g in range(0, ngroups - 2, 2):
        scores(g + 1, 1)
        attend(g, 0)
        scores(g + 2, 0)
        attend(g + 1, 1)
    scores(ngroups - 1, 1)
    attend(ngroups - 2, 0)
    attend(ngroups - 1, 1)


def _na_bias_table(rpb):
    nr, nc = 2 * NA_WIN_ROWS - 1, 2 * NA_WIN_COLS - 1
    cols = np.arange(GRID_W)
    cs = np.clip(cols - NA_WIN_COLS // 2, 0, GRID_W - NA_WIN_COLS)
    kc = np.arange(GRID_W)
    valid = (kc[None, :] >= cs[:, None]) & (kc[None, :] < cs[:, None] + NA_WIN_COLS)
    dc = kc[None, :] - cols[:, None] + (NA_WIN_COLS - 1)
    expand = (dc[None] == np.arange(nc)[:, None, None]).astype(np.float32).reshape(nc, -1)
    t = jnp.dot(rpb.astype(f32).reshape(-1, nc), expand, precision=lax.Precision.HIGHEST)
    t = t.reshape(NA_HEADS, nr, GRID_W, GRID_W)
    t = jnp.stack([t[:, NA_WIN_ROWS - 1 - off:2 * NA_WIN_ROWS - 1 - off] for off in range(NA_WIN_ROWS)])
    t = jnp.where(valid[None, None, None], t * LOG2E, NEG)
    t = jnp.transpose(t, (0, 1, 3, 2, 4))
    return t.reshape(NA_WIN_ROWS, NA_HEADS, GRID_W, NA_WIN_ROWS * GRID_W)


def _neighborhood_attention(a3, rpb, R=64):
    B, S, _ = a3.shape
    rows = S // GRID_W
    R = min(R, rows)
    assert rows >= NA_WIN_ROWS and rows % R == 0 and R % (2 * NA_GROUP) == 0 and R >= 4 * NA_GROUP
    bias = _na_bias_table(rpb)
    npair = NA_WIDTH // LANE
    return pl.pallas_call(
        functools.partial(_na_body, R=R, rows=rows),
        grid=(B, npair, rows // R),
        in_specs=[pl.BlockSpec((1, R * GRID_W, LANE), lambda b, p, i: (b, i, p)),
                  pl.BlockSpec((1, S, LANE), lambda b, p, i: (b, 0, npair + p)),
                  pl.BlockSpec((1, S, LANE), lambda b, p, i: (b, 0, 2 * npair + p)),
                  pl.BlockSpec((NA_WIN_ROWS, 2, GRID_W, NA_WIN_ROWS * GRID_W), lambda b, p, i: (0, p, 0, 0))],
        out_specs=pl.BlockSpec((1, R * GRID_W, LANE), lambda b, p, i: (b, i, p)),
        out_shape=jax.ShapeDtypeStruct((B, S, NA_WIDTH), bf16),
        scratch_shapes=[pltpu.VMEM((2, NA_GROUP, 2 * GRID_W, NA_WIN_ROWS * GRID_W), f32)],
        compiler_params=_cparams(("parallel", "parallel", "arbitrary")),
        name="neighborhood_attention",
    )(a3, a3, a3, bias)


def _mlstm_tables():
    L = CHUNK
    t = np.arange(L)
    tril = (t[None, :] <= t[:, None]).astype(np.float32)
    masks = np.stack([tril, tril.T, np.eye(L, dtype=np.float32)])
    rhs = np.zeros((2 * L, 2 * LANE), np.float32)
    rhs[:L, :LANE] = 1.0
    rhs[:L, LANE:] = -1.0
    rhs[L:, LANE:] = 1.0
    return masks, rhs


def _mlstm_body(qf, kf, vf, qb, kb, vb, gtf, gtb, bcol, msk_ref, rhs_ref, hf_ref, hb_ref,
                c_ref, m_ref, *, nch):
    L = CHUNK
    H = ML_HEADS
    dh = ML_HEAD_DIM
    scale = dh ** -0.5
    units = [(d, h) for d in range(2) for h in range(H)]

    @pl.when(pl.program_id(1) == 0)
    def _():
        c_ref[...] = jnp.zeros_like(c_ref)
        m_ref[...] = jnp.zeros_like(m_ref)

    sub = lax.broadcasted_iota(jnp.int32, (4 * H, 1), 0)
    f_row = (sub // H) % 2 == 1

    def chunk(j, carry):
        cjs = (j, nch - 1 - j)
        sts = [pl.multiple_of(cj * L, L) for cj in cjs]
        tril, triu, eye = msk_ref[0], msk_ref[1], msk_ref[2]
        rhs = rhs_ref[...]
        gts, brows = [], []
        for d, gt_ref in enumerate((gtf, gtb)):
            gt = gt_ref[0, cjs[d]] + bcol[...]
            gt = jnp.where(f_row, _log_sigmoid(gt), gt) * LOG2E
            gts.append(gt)
            brows.append(_dot_f32(gt, triu if d == 0 else tril))

        bcols, wcols = [], []
        for d, h in units:
            li_row = gts[d][d * 2 * H + h:d * 2 * H + h + 1, :]
            lf_row = gts[d][d * 2 * H + H + h:d * 2 * H + H + h + 1, :]
            lhs = jnp.concatenate([(tril if d == 0 else triu) * lf_row, eye * li_row], axis=1)
            hi = lhs.astype(bf16)
            lo = (lhs - hi.astype(f32)).astype(bf16)
            p = _dot(hi, rhs) + _dot(lo, rhs)
            bcols.append(p[:, :LANE])
            wcols.append(p[:, LANE:])

        qs, ks, vs, qks = [], [], [], []
        for d, h in units:
            q_ref, k_ref, v_ref = (qf, kf, vf) if d == 0 else (qb, kb, vb)
            hs = slice(h * dh, (h + 1) * dh)
            qs.append(q_ref[0, pl.ds(sts[d], L), hs])
            ks.append(k_ref[0, pl.ds(sts[d], L), hs])
            vs.append(v_ref[0, pl.ds(sts[d], L), hs])
            qks.append(_dot_nt(qs[-1], ks[-1]))

        scs, aiss, mts = [], [], []
        for u, (d, h) in enumerate(units):
            li_row = gts[d][d * 2 * H + h:d * 2 * H + h + 1, :]
            b_row = brows[d][d * 2 * H + H + h:d * 2 * H + H + h + 1, :]
            m_prev = m_ref[u]
            dm = jnp.where((tril if d == 0 else triu) > 0, bcols[u][:, :L] - b_row + li_row, NEG)
            inter = bcols[u] + m_prev
            mt = jnp.maximum(inter, jnp.max(dm, axis=-1, keepdims=True))
            scs.append(qks[u] * scale * jnp.exp2(dm - mt[:, :L]))
            aiss.append(jnp.exp2(inter - mt) * scale)
            mts.append(mt)

        nvs, qcs = [], []
        for u in range(len(units)):
            nvs.append(_dot(scs[u].astype(bf16), vs[u]))
            qcs.append(_dot(qs[u], c_ref[u].astype(bf16)))

        for u, (d, h) in enumerate(units):
            h_ref = hf_ref if d == 0 else hb_ref
            num = nvs[u] + aiss[u] * qcs[u][:, :dh]
            den = jnp.sum(scs[u], axis=-1, keepdims=True) + aiss[u] * qcs[u][:, dh:]
            h_ref[0, pl.ds(sts[d], L), h * dh:(h + 1) * dh] = (
                num / jnp.maximum(jnp.abs(den), jnp.exp2(-mts[u]))).astype(h_ref.dtype)

        for u, (d, h) in enumerate(units):
            m_prev = m_ref[u]
            bl = bcols[u][L - 1:L, :] if d == 0 else bcols[u][0:1, :]
            ds = wcols[u] + bl
            m_new = jnp.maximum(bl + m_prev, jnp.max(ds, axis=0, keepdims=True))
            decay = jnp.exp2(bl + m_prev - m_new)
            ws = jnp.exp2(ds - m_new)
            wv = jnp.concatenate([ws * vs[u].astype(f32), ws], axis=1).astype(bf16)
            c_ref[u] = jnp.concatenate([decay, decay], axis=1) * c_ref[u] + _dot_tn(ks[u], wv)
            m_ref[u] = m_new
        return carry

    lax.fori_loop(0, nch, chunk, 0, unroll=True)


def _mlstm(m4, gt, gate_b, tb=512):
    B, S, _ = m4.shape
    tb = min(tb, S)
    assert S % tb == 0 and tb % CHUNK == 0
    nb = S // tb
    nch = tb // CHUNK
    ng = 4 * ML_HEADS
    bcol = gate_b.astype(f32).reshape(ng, 1)
    masks_np, rhs_np = _mlstm_tables()
    masks = jnp.asarray(masks_np, f32)
    rhs = jnp.asarray(rhs_np, bf16)
    W = ML_WIDTH

    def fwd(c):
        return lambda b, i: (b, i, c)

    def bwd(c):
        return lambda b, i: (b, nb - 1 - i, c)

    in_specs = (
        [pl.BlockSpec((1, tb, W), fwd(c)) for c in range(3)]
        + [pl.BlockSpec((1, tb, W), bwd(c)) for c in range(3)]
        + [pl.BlockSpec((1, nch, ng, CHUNK), lambda b, i: (b, i, 0, 0)),
           pl.BlockSpec((1, nch, ng, CHUNK), lambda b, i: (b, nb - 1 - i, 0, 0)),
           pl.BlockSpec((ng, 1), lambda b, i: (0, 0)),
           pl.BlockSpec(masks.shape, lambda b, i: (0, 0, 0)),
           pl.BlockSpec(rhs.shape, lambda b, i: (0, 0))])
    return pl.pallas_call(
        functools.partial(_mlstm_body, nch=nch),
        grid=(B, nb),
        in_specs=in_specs,
        out_specs=[pl.BlockSpec((1, tb, W), fwd(0)), pl.BlockSpec((1, tb, W), bwd(0))],
        out_shape=[jax.ShapeDtypeStruct((B, S, W), bf16)] * 2,
        scratch_shapes=[pltpu.VMEM((2 * ML_HEADS, ML_HEAD_DIM, 2 * ML_HEAD_DIM), f32),
                        pltpu.VMEM((2 * ML_HEADS, 1, LANE), f32)],
        compiler_params=_cparams(("parallel", "arbitrary")),
        name="mlstm",
    )(m4, m4, m4, m4, m4, m4, gt, gt, bcol, masks, rhs)


def _even_out_body(x_ref, na_ref, hf_ref, hb_ref, o_ref, g_ref, w_ref, gp_ref, out_ref):
    hm = hf_ref[...].astype(f32) + hb_ref[...].astype(f32)
    hn = jnp.concatenate(
        [_rms(hm[:, h * ML_HEAD_DIM:(h + 1) * ML_HEAD_DIM]) for h in range(ML_HEADS)], axis=-1)
    y_ml = _sigmoid(o_ref[...].astype(f32)) * (hn * g_ref[...])
    m = _dot(na_ref[...], w_ref[:NA_WIDTH, :]) + _dot(y_ml.astype(bf16), w_ref[NA_WIDTH:, :])
    out_ref[...] = x_ref[...] + _rms(m) * gp_ref[...]


def _even_out(x2d, na2d, hf2d, hb2d, m4_2d, ml_g, w_out, g_post, tm=1024):
    T, Dm = x2d.shape
    W = ML_WIDTH
    row = lambda i: (i, 0)
    const = lambda i: (0, 0)
    return pl.pallas_call(
        _even_out_body,
        grid=(T // tm,),
        in_specs=[pl.BlockSpec((tm, Dm), row), pl.BlockSpec((tm, NA_WIDTH), row),
                  pl.BlockSpec((tm, W), row), pl.BlockSpec((tm, W), row),
                  pl.BlockSpec((tm, W), lambda i: (i, 3)),
                  pl.BlockSpec((1, W), const), pl.BlockSpec((NA_WIDTH + W, Dm), const),
                  pl.BlockSpec((1, Dm), const)],
        out_specs=pl.BlockSpec((tm, Dm), row),
        out_shape=jax.ShapeDtypeStruct((T, Dm), f32),
        compiler_params=_cparams(("parallel",)),
        name="even_out",
    )(x2d, na2d, hf2d, hb2d, m4_2d, ml_g.reshape(1, W), w_out, g_post.reshape(1, Dm))


GLA_BCAST_LEVELS = (32, 16, 8, 4)
T_SGN, T_COEF, T_PAIR = 0, 4, 8


def _gla_tables():
    L = CHUNK
    t = np.arange(L)
    tril = (t[None, :] <= t[:, None]).astype(np.float32)
    cum = np.stack([tril, tril.T])
    tbl = np.zeros((2, 12, L, LANE), np.float32)
    pm = np.zeros((2, 7, L, L), np.float32)
    for lv, g in enumerate((32, 16, 8, 4, 2, 1)):
        a = (t // (2 * g)) * 2 * g
        upper = (t - a) >= g
        same = a[:, None] == a[None, :]
        pm[0, lv] = same & upper[:, None] & (~upper)[None, :]
        pm[1, lv] = same & (~upper)[:, None] & upper[None, :]
        if g >= 4:
            tbl[0, T_SGN + lv] = np.where(upper, 1.0, -1.0)[:, None]
            tbl[1, T_SGN + lv] = np.where(upper, -1.0, 1.0)[:, None]
    pm[:, 6] = np.eye(L)
    r4, r2 = t % 4, t % 2
    tbl[0, T_COEF + 0] = np.isin(r4, (2, 3))[:, None]
    tbl[0, T_COEF + 1] = (r4 == 3)[:, None]
    tbl[0, T_COEF + 2] = (r4 == 0)[:, None]
    tbl[0, T_COEF + 3] = (r2 == 1)[:, None]
    tbl[1, T_COEF + 0] = np.isin(r4, (0, 1))[:, None]
    tbl[1, T_COEF + 1] = (r4 == 3)[:, None]
    tbl[1, T_COEF + 2] = (r4 == 0)[:, None]
    tbl[1, T_COEF + 3] = (r2 == 0)[:, None]
    for p in range(4):
        tbl[:, T_PAIR + p, :, :L] = pm[:, 2 * p]
        if p < 3:
            tbl[:, T_PAIR + p, :, L:] = pm[:, 2 * p + 1]
    return cum, tbl


def _gla_body(qf, kf, vf, gdf, qb, kb, vb, gdb, gu_ref, gbias_ref, cum_ref, tbl_ref,
              of_ref, ob_ref, s_ref, *, nch):
    L = CHUNK
    H = GLA_HEADS
    dk = GLA_DK
    dv = GLA_DV
    scale = dk ** -0.5
    units = [(d, h) for d in range(2) for h in range(H)]

    @pl.when(pl.program_id(1) == 0)
    def _():
        s_ref[...] = jnp.zeros_like(s_ref)

    def chunk(j, carry):
        cjs = (j, nch - 1 - j)
        sts = [pl.multiple_of(cj * L, L) for cj in cjs]
        zk = jnp.zeros((L, dk), bf16)

        las, bs = [], []
        for d, gd_ref in enumerate((gdf, gdb)):
            gd = gd_ref[0, pl.ds(sts[d], L), :].astype(bf16)
            la = _log_sigmoid(_dot(gd, gu_ref[d]) + gbias_ref[d]) * (LOG2E / GLA_TAU)
            hi = la.astype(bf16)
            lo = (la - hi.astype(f32)).astype(bf16)
            las.append(la)
            bs.append(_dot(cum_ref[d], hi) + _dot(cum_ref[d], lo))

        qss, kfs, vs, rs = [], [], [], []
        for d, h in units:
            q_ref, k_ref, v_ref = (qf, kf, vf) if d == 0 else (qb, kb, vb)
            ks = slice(h * dk, (h + 1) * dk)
            la = las[d][:, ks]
            b = bs[d][:, ks]
            k = k_ref[0, pl.ds(sts[d], L), ks]
            qs = q_ref[0, pl.ds(sts[d], L), ks].astype(f32) * scale
            kf32 = k.astype(f32)
            xs = []
            for lv, g in enumerate(GLA_BCAST_LEVELS):
                refs = [a + g - 1 + d for a in range(0, L, 2 * g)]
                bref = jnp.concatenate(
                    [jnp.broadcast_to(b[r:r + 1, :], (2 * g, dk)) for r in refs], axis=0)
                xs.append(jnp.exp2((b - bref) * tbl_ref[d, T_SGN + lv]))
            la_dn = pltpu.roll(la, 1, axis=0)
            la_up = pltpu.roll(la, L - 1, axis=0)
            xs.append(jnp.exp2(la * tbl_ref[d, T_COEF] + la_dn * tbl_ref[d, T_COEF + 1]
                               + la_up * tbl_ref[d, T_COEF + 2]))
            xs.append(jnp.exp2(la * tbl_ref[d, T_COEF + 3]))
            r_u = []
            for p in range(3):
                xa, xb = xs[2 * p], xs[2 * p + 1]
                lhs = jnp.concatenate([(qs * xa).astype(bf16), (qs * xb).astype(bf16)], axis=1)
                rhs = jnp.concatenate(
                    [jnp.concatenate([(kf32 * xa).astype(bf16), zk], axis=1),
                     jnp.concatenate([zk, (kf32 * xb).astype(bf16)], axis=1)], axis=0)
                r_u.append(_dot_nt(lhs, rhs))
            r_u.append(_dot_nt(qs.astype(bf16), jnp.concatenate([k, zk], axis=0)))
            rs.append(r_u)
            qss.append(qs)
            kfs.append(kf32)
            vs.append(v_ref[0, pl.ds(sts[d], L), h * dv:(h + 1) * dv])

        for u, (d, h) in enumerate(units):
            o_ref = of_ref if d == 0 else ob_ref
            b = bs[d][:, h * dk:(h + 1) * dk]
            a2 = jnp.where(tbl_ref[d, T_PAIR + 3] > 0, rs[u][3], 0.0)
            for p in range(2, -1, -1):
                a2 = jnp.where(tbl_ref[d, T_PAIR + p] > 0, rs[u][p], a2)
            v2 = jnp.concatenate([vs[u], vs[u]], axis=0)
            o = _dot(a2.astype(bf16), v2) + _dot_nt((qss[u] * jnp.exp2(b)).astype(bf16),
                                                    s_ref[u].astype(bf16))
            o_ref[0, pl.ds(sts[d], L), h * dv:(h + 1) * dv] = o.astype(o_ref.dtype)

        for u, (d, h) in enumerate(units):
            b = bs[d][:, h * dk:(h + 1) * dk]
            bl = b[L - 1:L, :] if d == 0 else b[0:1, :]
            kt = (kfs[u] * jnp.exp2(bl - b)).astype(bf16)
            s_ref[u] = s_ref[u] * jnp.exp2(bl) + _dot_tn(vs[u], kt)
        return carry

    lax.fori_loop(0, nch, chunk, 0, unroll=True)


def _gla(qk, v, gd, gate_up, gate_b, tb=512):
    B, S, _ = qk.shape
    tb = min(tb, S)
    assert S % tb == 0 and tb % CHUNK == 0
    nb = S // tb
    nch = tb // CHUNK
    R = GLA_GATE_RANK
    gu = jnp.zeros((2, LANE, GLA_KW), f32)
    gu = gu.at[0, :R].set(gate_up[0]).at[1, R:2 * R].set(gate_up[1]).astype(bf16)
    gbias = gate_b.astype(f32).reshape(2, 1, GLA_KW)
    cum_np, tbl_np = _gla_tables()
    cum = jnp.asarray(cum_np, bf16)
    tbl = jnp.asarray(tbl_np, f32)

    def fwd(c):
        return lambda b, i: (b, i, c)

    def bwd(c):
        return lambda b, i: (b, nb - 1 - i, c)

    const3 = lambda b, i: (0, 0, 0)
    in_specs = []
    for mk in (fwd, bwd):
        in_specs += [pl.BlockSpec((1, tb, GLA_KW), mk(0)), pl.BlockSpec((1, tb, GLA_KW), mk(1)),
                     pl.BlockSpec((1, tb, GLA_VW), mk(0)), pl.BlockSpec((1, tb, LANE), mk(0))]
    in_specs += [pl.BlockSpec((2, LANE, GLA_KW), const3), pl.BlockSpec((2, 1, GLA_KW), const3),
                 pl.BlockSpec(cum.shape, const3), pl.BlockSpec(tbl.shape, lambda b, i: (0, 0, 0, 0))]
    return pl.pallas_call(
        functools.partial(_gla_body, nch=nch),
        grid=(B, nb),
        in_specs=in_specs,
        out_specs=[pl.BlockSpec((1, tb, GLA_VW), fwd(0)), pl.BlockSpec((1, tb, GLA_VW), bwd(0))],
        out_shape=[jax.ShapeDtypeStruct((B, S, GLA_VW), bf16)] * 2,
        scratch_shapes=[pltpu.VMEM((2 * GLA_HEADS, GLA_DV, GLA_DK), f32)],
        compiler_params=_cparams(("parallel", "arbitrary")),
        name="gla",
    )(qk, qk, v, gd, qk, qk, v, gd, gu, gbias, cum, tbl)


def _odd_out_body(x_ref, of_ref, ob_ref, r_ref, g_ref, w_ref, gp_ref, out_ref):
    o = of_ref[...].astype(f32) + ob_ref[...].astype(f32)
    hn = jnp.concatenate(
        [_rms(o[:, h * GLA_DV:(h + 1) * GLA_DV]) for h in range(GLA_HEADS)], axis=-1)
    r = r_ref[...].astype(f32)
    y = (hn * g_ref[...]) * (r * _sigmoid(r))
    m = _dot(y.astype(bf16), w_ref[...])
    out_ref[...] = x_ref[...] + _rms(m) * gp_ref[...]


def _odd_out(x2d, of2d, ob2d, r2d, norm_g, w_out, g_post, tm=1024):
    T, Dm = x2d.shape
    row = lambda i: (i, 0)
    const = lambda i: (0, 0)
    return pl.pallas_call(
        _odd_out_body,
        grid=(T // tm,),
        in_specs=[pl.BlockSpec((tm, Dm), row), pl.BlockSpec((tm, GLA_VW), row),
                  pl.BlockSpec((tm, GLA_VW), row), pl.BlockSpec((tm, GLA_VW), row),
                  pl.BlockSpec((1, GLA_VW), const), pl.BlockSpec((GLA_VW, Dm), const),
                  pl.BlockSpec((1, Dm), const)],
        out_specs=pl.BlockSpec((tm, Dm), row),
        out_shape=jax.ShapeDtypeStruct((T, Dm), f32),
        compiler_params=_cparams(("parallel",)),
        name="odd_out",
    )(x2d, of2d, ob2d, r2d, norm_g.reshape(1, GLA_VW), w_out, g_post.reshape(1, Dm))


FFN_HALO = 8
FFN_COLS = 256
GELU_K2 = 2.0 * 0.7978845608028654
GELU_C = 0.044715


def _ffn_body(x_ref, xn_ref, xp_ref, gpre_ref, wu_ref, cw_ref, cb_ref, wd_ref, gpost_ref, p_ref,
              wg_ref, wp_ref, out_ref, lhs_ref, ue_ref, acc_ref, *, tm):
    i = pl.program_id(1)
    keep_prev = (i > 0).astype(f32)
    keep_next = (i < pl.num_programs(1) - 1).astype(f32)
    gpre = gpre_ref[...]
    x = x_ref[0]
    halo = jnp.concatenate([_rms(xn_ref[0]) * gpre * keep_next, _rms(xp_ref[0]) * gpre * keep_prev], axis=0)
    lhs_ref[...] = jnp.concatenate([(_rms(x) * gpre).astype(bf16), halo.astype(bf16)], axis=0)
    acc_ref[...] = jnp.zeros_like(acc_ref)
    te = tm + 2 * FFN_HALO
    nchunks = D_FF // FFN_COLS

    def cols(base, n):
        return pl.ds(pl.multiple_of(base + n * FFN_COLS, FFN_COLS), FFN_COLS)

    def up(n, slot):
        lhs = lhs_ref[...]
        ue_ref[slot, 0] = _dot(lhs, wu_ref[:, cols(D_FF, n)])
        ue_ref[slot, 1] = _dot(lhs, wu_ref[:, cols(0, n)])

    def conv(n, slot, half):
        ue = ue_ref[slot, half]
        cs = cols(D_FF * (1 - half), n)
        w = cw_ref[:, cs]
        above = pltpu.roll(ue, 1, axis=0)[:tm]
        below = pltpu.roll(ue, te - 1, axis=0)[:tm]
        return above * w[0:1] + ue[:tm] * w[1:2] + below * w[2:3] + cb_ref[:, cs]

    def down(n, slot):
        g = conv(n, slot, 0)
        e = jnp.exp2(g * (g * g * (-GELU_K2 * GELU_C * LOG2E) + (-GELU_K2 * LOG2E)))
        act = (g / (1.0 + e) * conv(n, slot, 1)).astype(bf16)
        acc_ref[...] += _dot(act, wd_ref[cols(0, n), :])

    up(0, 0)

    def pair(jj, carry):
        n = 2 * jj
        up(n + 1, 1)
        down(n, 0)
        up(n + 2, 0)
        down(n + 1, 1)
        return carry

    lax.fori_loop(0, (nchunks - 1) // 2, pair, 0)
    down(nchunks - 1, 0)
    x2 = x + _rms(acc_ref[...]) * gpost_ref[...]
    gate = _sigmoid(_dot(x2.astype(bf16), wg_ref[...]))
    out_ref[0] = x2 + _dot(p_ref[0, 0].astype(bf16), wp_ref[...]) * gate


def _resident(shape):
    nd = len(shape)
    return pl.BlockSpec(shape, lambda b, i: (0,) * nd, pipeline_mode=pl.Buffered(1))


def _ffn(x3, g_pre, w_up, conv_w, conv_b, w_down, g_post, p4, layer, w_gate, w_proj, tm=512):
    B, S, Dm = x3.shape
    NF = w_up.shape[1]
    PD = p4.shape[-1]
    tm = min(tm, S)
    nchunks = D_FF // FFN_COLS
    assert S % tm == 0 and tm % FFN_HALO == 0 and D_FF % FFN_COLS == 0 and NF == 2 * D_FF and nchunks % 2 == 1
    hb = tm // FFN_HALO
    nhalo = S // FFN_HALO
    te = tm + 2 * FFN_HALO
    blk = lambda b, i: (b, i, 0)
    return pl.pallas_call(
        functools.partial(_ffn_body, tm=tm),
        grid=(B, S // tm),
        in_specs=[pl.BlockSpec((1, tm, Dm), blk),
                  pl.BlockSpec((1, FFN_HALO, Dm), lambda b, i: (b, jnp.minimum((i + 1) * hb, nhalo - 1), 0)),
                  pl.BlockSpec((1, FFN_HALO, Dm), lambda b, i: (b, jnp.maximum(i * hb - 1, 0), 0)),
                  _resident((1, Dm)), _resident((Dm, NF)), _resident((3, NF)), _resident((1, NF)),
                  _resident((D_FF, Dm)), _resident((1, Dm)),
                  pl.BlockSpec((1, 1, tm, PD), lambda b, i: (layer, b, i, 0)),
                  _resident((Dm, Dm)), _resident((PD, Dm))],
        out_specs=pl.BlockSpec((1, tm, Dm), blk),
        out_shape=jax.ShapeDtypeStruct((B, S, Dm), f32),
        scratch_shapes=[pltpu.VMEM((te, Dm), bf16),
                        pltpu.VMEM((2, 2, te, FFN_COLS), f32),
                        pltpu.VMEM((tm, Dm), f32)],
        compiler_params=_cparams(("parallel", "parallel")),
        name="ffn",
    )(x3, x3, x3, g_pre.reshape(1, Dm), w_up, conv_w, conv_b.reshape(1, NF), w_down,
      g_post.reshape(1, Dm), p4, w_gate, w_proj)


def _pad_cols(w, n):
    return jnp.pad(w, ((0, 0), (0, n - w.shape[1])))


def _trunk(x, p, norm_mix_pre, norm_mix_post, norm_ffn_pre, norm_ffn_post,
           even_w_in, even_na_rpb, even_ml_gate_b, even_ml_norm_g, even_w_out,
           odd_w_in, odd_gate_up, odd_gate_b, odd_norm_g, odd_w_out,
           ffn_w_up, ffn_conv_w, ffn_conv_b, ffn_w_down, ple_w_proj, ple_w_gate):
    B, S, Dm = x.shape
    T = B * S
    depth = norm_mix_pre.shape[0]
    for i in range(depth):
        j = i // 2
        x2d = x.reshape(T, Dm)
        if i % 2 == 0:
            widths = (3 * NA_WIDTH, 4 * ML_WIDTH)
            w_in = even_w_in[j].astype(bf16)
            a3, m4, gt = _norm_matmul(x2d, norm_mix_pre[i], w_in[:, :sum(widths)], widths, (bf16, bf16),
                                      wt=w_in[:, sum(widths):].T)
            y_na = _neighborhood_attention(a3.reshape(B, S, -1), even_na_rpb[j])
            gt = jnp.transpose(gt.reshape(-1, B, S // CHUNK, CHUNK), (1, 2, 0, 3))
            hf, hb = _mlstm(m4.reshape(B, S, -1), gt, even_ml_gate_b[j])
            x2d = _even_out(x2d, y_na.reshape(T, -1), hf.reshape(T, -1), hb.reshape(T, -1), m4,
                            even_ml_norm_g[j], even_w_out[j].astype(bf16), norm_mix_post[i])
        else:
            widths = (2 * GLA_KW, GLA_VW, GLA_VW, LANE)
            w_in = _pad_cols(odd_w_in[j], sum(widths)).astype(bf16)
            qk, v, r, gd = _norm_matmul(x2d, norm_mix_pre[i], w_in, widths, (bf16, bf16, bf16, f32))
            of, ob = _gla(qk.reshape(B, S, -1), v.reshape(B, S, -1), gd.reshape(B, S, LANE),
                          odd_gate_up[j], odd_gate_b[j])
            x2d = _odd_out(x2d, of.reshape(T, -1), ob.reshape(T, -1), r, odd_norm_g[j],
                           odd_w_out[j].astype(bf16), norm_mix_post[i])
        x = _ffn(x2d.reshape(B, S, Dm), norm_ffn_pre[i], ffn_w_up[i].astype(bf16), ffn_conv_w[i],
                 ffn_conv_b[i], ffn_w_down[i].astype(bf16), norm_ffn_post[i], p, i,
                 ple_w_gate[i].astype(bf16), ple_w_proj[i].astype(bf16))
    return x


def kernel(x_prompt, x_sample, p_prompt, p_sample, norm_mix_pre, norm_mix_post, norm_ffn_pre, norm_ffn_post, even_w_in, even_na_rpb, even_ml_gate_b, even_ml_norm_g, even_w_out, odd_w_in, odd_gate_up, odd_gate_b, odd_norm_g, odd_w_out, ffn_w_up, ffn_conv_w, ffn_conv_b, ffn_w_down, ple_w_proj, ple_w_gate):
    params = (norm_mix_pre, norm_mix_post, norm_ffn_pre, norm_ffn_post,
              even_w_in, even_na_rpb, even_ml_gate_b, even_ml_norm_g, even_w_out,
              odd_w_in, odd_gate_up, odd_gate_b, odd_norm_g, odd_w_out,
              ffn_w_up, ffn_conv_w, ffn_conv_b, ffn_w_down, ple_w_proj, ple_w_gate)
    return (_trunk(x_prompt, p_prompt, *params), _trunk(x_sample, p_sample, *params))
```

```python
import functools

import jax
import jax.numpy as jnp
import numpy as np
from jax import lax
from jax.experimental import pallas as pl
from jax.experimental.pallas import tpu as pltpu

f32 = jnp.float32
bf16 = jnp.bfloat16

D_MODEL = 1024
EPS = 1e-6
GRID_W = 64
NA_HEADS = 8
NA_HEAD_DIM = 64
NA_WIN_ROWS = 8
NA_WIN_COLS = 16
NA_GROUP = 4
NA_WIDTH = NA_HEADS * NA_HEAD_DIM
ML_HEADS = 4
ML_HEAD_DIM = 128
ML_WIDTH = ML_HEADS * ML_HEAD_DIM
GLA_HEADS = 4
GLA_DK = 128
GLA_DV = 256
GLA_KW = GLA_HEADS * GLA_DK
GLA_VW = GLA_HEADS * GLA_DV
GLA_GATE_RANK = 16
GLA_TAU = 16.0
CHUNK = 64
D_FF = 2816
LANE = 128
NEG = -1e30
LOG2E = 1.4426950408889634

VMEM_LIMIT = 56 * 1024 * 1024


def _cparams(sem):
    return pltpu.CompilerParams(dimension_semantics=sem, vmem_limit_bytes=VMEM_LIMIT)


def _dot(a, b):
    return jnp.dot(a, b, preferred_element_type=f32)


def _dot_nt(a, b):
    return lax.dot_general(a, b, (((1,), (1,)), ((), ())), preferred_element_type=f32)


def _dot_tn(a, b):
    return lax.dot_general(a, b, (((0,), (0,)), ((), ())), preferred_element_type=f32)


def _dot_f32(a, b):
    return jnp.dot(a, b, precision=lax.Precision.HIGHEST, preferred_element_type=f32)


def _rms(x):
    return x * lax.rsqrt(jnp.mean(x * x, axis=-1, keepdims=True) + EPS)


def _sigmoid(x):
    return 1.0 / (1.0 + jnp.exp(-x))


def _log_sigmoid(x):
    return jnp.minimum(x, 0.0) - jnp.log(1.0 + jnp.exp(-jnp.abs(x)))


def _norm_matmul_body(x_ref, g_ref, w_ref, *refs, widths, with_t):
    hn = (_rms(x_ref[...]) * g_ref[...]).astype(bf16)
    o_refs = refs[1:] if with_t else refs
    off = 0
    for o_ref, n in zip(o_refs, widths):
        for c0 in range(0, n, 512):
            cw = min(512, n - c0)
            o_ref[:, c0:c0 + cw] = _dot(hn, w_ref[:, off + c0:off + c0 + cw]).astype(o_ref.dtype)
        off += n
    if with_t:
        o_refs[len(widths)][...] = _dot_nt(refs[0][...], hn)


def _norm_matmul(x2d, g, w, widths, dtypes, wt=None, tm=512):
    T, Dm = x2d.shape
    N = w.shape[1]
    assert sum(widths) == N and T % tm == 0
    const = lambda i: (0, 0)
    in_specs = [pl.BlockSpec((tm, Dm), lambda i: (i, 0)), pl.BlockSpec((1, Dm), const), pl.BlockSpec((Dm, N), const)]
    out_specs = [pl.BlockSpec((tm, n), lambda i: (i, 0)) for n in widths]
    out_shape = [jax.ShapeDtypeStruct((T, n), dt) for n, dt in zip(widths, dtypes)]
    args = [x2d, g.reshape(1, Dm), w]
    if wt is not None:
        in_specs.append(pl.BlockSpec(wt.shape, const))
        out_specs.append(pl.BlockSpec((wt.shape[0], tm), lambda i: (0, i)))
        out_shape.append(jax.ShapeDtypeStruct((wt.shape[0], T), f32))
        args.append(wt)
    return pl.pallas_call(
        functools.partial(_norm_matmul_body, widths=tuple(widths), with_t=wt is not None),
        grid=(T // tm,),
        in_specs=in_specs,
        out_specs=out_specs,
        out_shape=out_shape,
        compiler_params=_cparams(("parallel",)),
        name="norm_matmul",
    )(*args)


def _na_body(q_ref, k_ref, v_ref, bias_ref, o_ref, s_ref, *, R, rows):
    i = pl.program_id(2)
    lane = lax.broadcasted_iota(jnp.int32, (GRID_W, LANE), 1)
    first = lane < NA_HEAD_DIM
    nkeys = NA_WIN_ROWS * GRID_W
    ngroups = R // NA_GROUP

    def row_info(g, t):
        j = g * NA_GROUP + t
        r = i * R + j
        rs = jnp.clip(r - NA_WIN_ROWS // 2, 0, rows - NA_WIN_ROWS)
        return (r - rs, pl.multiple_of(j * GRID_W, GRID_W), pl.multiple_of(rs * GRID_W, GRID_W))

    def scores(g, slot):
        for t in range(NA_GROUP):
            case, qoff, kstart = row_info(g, t)
            q = q_ref[0, pl.ds(qoff, GRID_W), :]
            zero = jnp.zeros_like(q)
            q2 = jnp.concatenate([jnp.where(first, q, zero), jnp.where(first, zero, q)], axis=0)
            kw = k_ref[0, pl.ds(kstart, nkeys), :]
            s = _dot_nt(q2, kw) * (NA_HEAD_DIM ** -0.5 * LOG2E)
            s_ref[slot, t] = s + jnp.concatenate([bias_ref[case, 0], bias_ref[case, 1]], axis=0)

    def attend(g, slot):
        p_all, l_all = [], []
        for t in range(NA_GROUP):
            s = s_ref[slot, t]
            p = jnp.exp2(s - jnp.max(s, axis=-1, keepdims=True))
            l_all.append(jnp.sum(p, axis=-1, keepdims=True))
            p_all.append(p.astype(bf16))
        for t in range(NA_GROUP):
            _, qoff, kstart = row_info(g, t)
            vw = v_ref[0, pl.ds(kstart, nkeys), :]
            o = _dot(p_all[t], vw) / l_all[t]
            o_ref[0, pl.ds(qoff, GRID_W), :] = jnp.where(
                first, o[:GRID_W], o[GRID_W:]).astype(o_ref.dtype)

    scores(0, 0)

    for g in range(0, ngroups - 2, 2):
        scores(g + 1, 1)
        attend(g, 0)
        scores(g + 2, 0)
        attend(g + 1, 1)
    scores(ngroups - 1, 1)
    attend(ngroups - 2, 0)
    attend(ngroups - 1, 1)


def _na_bias_table(rpb):
    nr, nc = 2 * NA_WIN_ROWS - 1, 2 * NA_WIN_COLS - 1
    cols = np.arange(GRID_W)
    cs = np.clip(cols - NA_WIN_COLS // 2, 0, GRID_W - NA_WIN_COLS)
    kc = np.arange(GRID_W)
    valid = (kc[None, :] >= cs[:, None]) & (kc[None, :] < cs[:, None] + NA_WIN_COLS)
    dc = kc[None, :] - cols[:, None] + (NA_WIN_COLS - 1)
    expand = (dc[None] == np.arange(nc)[:, None, None]).astype(np.float32).reshape(nc, -1)
    t = jnp.dot(rpb.astype(f32).reshape(-1, nc), expand, precision=lax.Precision.HIGHEST)
    t = t.reshape(NA_HEADS, nr, GRID_W, GRID_W)
    t = jnp.stack([t[:, NA_WIN_ROWS - 1 - off:2 * NA_WIN_ROWS - 1 - off] for off in range(NA_WIN_ROWS)])
    t = jnp.where(valid[None, None, None], t * LOG2E, NEG)
    t = jnp.transpose(t, (0, 1, 3, 2, 4))
    return t.reshape(NA_WIN_ROWS, NA_HEADS, GRID_W, NA_WIN_ROWS * GRID_W)


def _neighborhood_attention(a3, rpb, R=64):
    B, S, _ = a3.shape
    rows = S // GRID_W
    R = min(R, rows)
    assert rows >= NA_WIN_ROWS and rows % R == 0 and R % (2 * NA_GROUP) == 0 and R >= 4 * NA_GROUP
    bias = _na_bias_table(rpb)
    npair = NA_WIDTH // LANE
    return pl.pallas_call(
        functools.partial(_na_body, R=R, rows=rows),
        grid=(B, npair, rows // R),
        in_specs=[pl.BlockSpec((1, R * GRID_W, LANE), lambda b, p, i: (b, i, p)),
                  pl.BlockSpec((1, S, LANE), lambda b, p, i: (b, 0, npair + p)),
                  pl.BlockSpec((1, S, LANE), lambda b, p, i: (b, 0, 2 * npair + p)),
                  pl.BlockSpec((NA_WIN_ROWS, 2, GRID_W, NA_WIN_ROWS * GRID_W), lambda b, p, i: (0, p, 0, 0))],
        out_specs=pl.BlockSpec((1, R * GRID_W, LANE), lambda b, p, i: (b, i, p)),
        out_shape=jax.ShapeDtypeStruct((B, S, NA_WIDTH), bf16),
        scratch_shapes=[pltpu.VMEM((2, NA_GROUP, 2 * GRID_W, NA_WIN_ROWS * GRID_W), f32)],
        compiler_params=_cparams(("parallel", "parallel", "arbitrary")),
        name="neighborhood_attention",
    )(a3, a3, a3, bias)


def _mlstm_tables():
    L = CHUNK
    t = np.arange(L)
    tril = (t[None, :] <= t[:, None]).astype(np.float32)
    masks = np.stack([tril, tril.T, np.eye(L, dtype=np.float32)])
    rhs = np.zeros((2 * L, 2 * LANE), np.float32)
    rhs[:L, :LANE] = 1.0
    rhs[:L, LANE:] = -1.0
    rhs[L:, LANE:] = 1.0
    return masks, rhs


def _mlstm_body(qf, kf, vf, qb, kb, vb, gtf, gtb, bcol, msk_ref, rhs_ref, hf_ref, hb_ref,
                c_ref, m_ref, *, nch):
    L = CHUNK
    H = ML_HEADS
    dh = ML_HEAD_DIM
    scale = dh ** -0.5
    units = [(d, h) for d in range(2) for h in range(H)]

    @pl.when(pl.program_id(1) == 0)
    def _():
        c_ref[...] = jnp.zeros_like(c_ref)
        m_ref[...] = jnp.zeros_like(m_ref)

    sub = lax.broadcasted_iota(jnp.int32, (4 * H, 1), 0)
    f_row = (sub // H) % 2 == 1

    def chunk(j, carry):
        cjs = (j, nch - 1 - j)
        sts = [pl.multiple_of(cj * L, L) for cj in cjs]
        tril, triu, eye = msk_ref[0], msk_ref[1], msk_ref[2]
        rhs = rhs_ref[...]
        gts, brows = [], []
        for d, gt_ref in enumerate((gtf, gtb)):
            gt = gt_ref[0, cjs[d]] + bcol[...]
            gt = jnp.where(f_row, _log_sigmoid(gt), gt) * LOG2E
            gts.append(gt)
            brows.append(_dot_f32(gt, triu if d == 0 else tril))

        bcols, wcols = [], []
        for d, h in units:
            li_row = gts[d][d * 2 * H + h:d * 2 * H + h + 1, :]
            lf_row = gts[d][d * 2 * H + H + h:d * 2 * H + H + h + 1, :]
            lhs = jnp.concatenate([(tril if d == 0 else triu) * lf_row, eye * li_row], axis=1)
            hi = lhs.astype(bf16)
            lo = (lhs - hi.astype(f32)).astype(bf16)
            p = _dot(hi, rhs) + _dot(lo, rhs)
            bcols.append(p[:, :LANE])
            wcols.append(p[:, LANE:])

        qs, ks, vs, qks = [], [], [], []
        for d, h in units:
            q_ref, k_ref, v_ref = (qf, kf, vf) if d == 0 else (qb, kb, vb)
            hs = slice(h * dh, (h + 1) * dh)
            qs.append(q_ref[0, pl.ds(sts[d], L), hs])
            ks.append(k_ref[0, pl.ds(sts[d], L), hs])
            vs.append(v_ref[0, pl.ds(sts[d], L), hs])
            qks.append(_dot_nt(qs[-1], ks[-1]))

        scs, aiss, mts = [], [], []
        for u, (d, h) in enumerate(units):
            li_row = gts[d][d * 2 * H + h:d * 2 * H + h + 1, :]
            b_row = brows[d][d * 2 * H + H + h:d * 2 * H + H + h + 1, :]
            m_prev = m_ref[u]
            dm = jnp.where((tril if d == 0 else triu) > 0, bcols[u][:, :L] - b_row + li_row, NEG)
            inter = bcols[u] + m_prev
            mt = jnp.maximum(inter, jnp.max(dm, axis=-1, keepdims=True))
            scs.append(qks[u] * scale * jnp.exp2(dm - mt[:, :L]))
            aiss.append(jnp.exp2(inter - mt) * scale)
            mts.append(mt)

        nvs, qcs = [], []
        for u in range(len(units)):
            nvs.append(_dot(scs[u].astype(bf16), vs[u]))
            qcs.append(_dot(qs[u], c_ref[u].astype(bf16)))

        for u, (d, h) in enumerate(units):
            h_ref = hf_ref if d == 0 else hb_ref
            num = nvs[u] + aiss[u] * qcs[u][:, :dh]
            den = jnp.sum(scs[u], axis=-1, keepdims=True) + aiss[u] * qcs[u][:, dh:]
            h_ref[0, pl.ds(sts[d], L), h * dh:(h + 1) * dh] = (
                num / jnp.maximum(jnp.abs(den), jnp.exp2(-mts[u]))).astype(h_ref.dtype)

        for u, (d, h) in enumerate(units):
            m_prev = m_ref[u]
            bl = bcols[u][L - 1:L, :] if d == 0 else bcols[u][0:1, :]
            ds = wcols[u] + bl
            m_new = jnp.maximum(bl + m_prev, jnp.max(ds, axis=0, keepdims=True))
            decay = jnp.exp2(bl + m_prev - m_new)
            ws = jnp.exp2(ds - m_new)
            wv = jnp.concatenate([ws * vs[u].astype(f32), ws], axis=1).astype(bf16)
            c_ref[u] = jnp.concatenate([decay, decay], axis=1) * c_ref[u] + _dot_tn(ks[u], wv)
            m_ref[u] = m_new
        return carry

    lax.fori_loop(0, nch, chunk, 0, unroll=True)


def _mlstm(m4, gt, gate_b, tb=512):
    B, S, _ = m4.shape
    tb = min(tb, S)
    assert S % tb == 0 and tb % CHUNK == 0
    nb = S // tb
    nch = tb // CHUNK
    ng = 4 * ML_HEADS
    bcol = gate_b.astype(f32).reshape(ng, 1)
    masks_np, rhs_np = _mlstm_tables()
    masks = jnp.asarray(masks_np, f32)
    rhs = jnp.asarray(rhs_np, bf16)
    W = ML_WIDTH

    def fwd(c):
        return lambda b, i: (b, i, c)

    def bwd(c):
        return lambda b, i: (b, nb - 1 - i, c)

    in_specs = (
        [pl.BlockSpec((1, tb, W), fwd(c)) for c in range(3)]
        + [pl.BlockSpec((1, tb, W), bwd(c)) for c in range(3)]
        + [pl.BlockSpec((1, nch, ng, CHUNK), lambda b, i: (b, i, 0, 0)),
           pl.BlockSpec((1, nch, ng, CHUNK), lambda b, i: (b, nb - 1 - i, 0, 0)),
           pl.BlockSpec((ng, 1), lambda b, i: (0, 0)),
           pl.BlockSpec(masks.shape, lambda b, i: (0, 0, 0)),
           pl.BlockSpec(rhs.shape, lambda b, i: (0, 0))])
    return pl.pallas_call(
        functools.partial(_mlstm_body, nch=nch),
        grid=(B, nb),
        in_specs=in_specs,
        out_specs=[pl.BlockSpec((1, tb, W), fwd(0)), pl.BlockSpec((1, tb, W), bwd(0))],
        out_shape=[jax.ShapeDtypeStruct((B, S, W), bf16)] * 2,
        scratch_shapes=[pltpu.VMEM((2 * ML_HEADS, ML_HEAD_DIM, 2 * ML_HEAD_DIM), f32),
                        pltpu.VMEM((2 * ML_HEADS, 1, LANE), f32)],
        compiler_params=_cparams(("parallel", "arbitrary")),
        name="mlstm",
    )(m4, m4, m4, m4, m4, m4, gt, gt, bcol, masks, rhs)


def _even_out_body(x_ref, na_ref, hf_ref, hb_ref, o_ref, g_ref, w_ref, gp_ref, out_ref):
    hm = hf_ref[...].astype(f32) + hb_ref[...].astype(f32)
    hn = jnp.concatenate(
        [_rms(hm[:, h * ML_HEAD_DIM:(h + 1) * ML_HEAD_DIM]) for h in range(ML_HEADS)], axis=-1)
    y_ml = _sigmoid(o_ref[...].astype(f32)) * (hn * g_ref[...])
    m = _dot(na_ref[...], w_ref[:NA_WIDTH, :]) + _dot(y_ml.astype(bf16), w_ref[NA_WIDTH:, :])
    out_ref[...] = x_ref[...] + _rms(m) * gp_ref[...]


def _even_out(x2d, na2d, hf2d, hb2d, m4_2d, ml_g, w_out, g_post, tm=1024):
    T, Dm = x2d.shape
    W = ML_WIDTH
    row = lambda i: (i, 0)
    const = lambda i: (0, 0)
    return pl.pallas_call(
        _even_out_body,
        grid=(T // tm,),
        in_specs=[pl.BlockSpec((tm, Dm), row), pl.BlockSpec((tm, NA_WIDTH), row),
                  pl.BlockSpec((tm, W), row), pl.BlockSpec((tm, W), row),
                  pl.BlockSpec((tm, W), lambda i: (i, 3)),
                  pl.BlockSpec((1, W), const), pl.BlockSpec((NA_WIDTH + W, Dm), const),
                  pl.BlockSpec((1, Dm), const)],
        out_specs=pl.BlockSpec((tm, Dm), row),
        out_shape=jax.ShapeDtypeStruct((T, Dm), f32),
        compiler_params=_cparams(("parallel",)),
        name="even_out",
    )(x2d, na2d, hf2d, hb2d, m4_2d, ml_g.reshape(1, W), w_out, g_post.reshape(1, Dm))


GLA_BCAST_LEVELS = (32, 16, 8, 4)
T_SGN, T_COEF, T_PAIR = 0, 4, 8


def _gla_tables():
    L = CHUNK
    t = np.arange(L)
    tril = (t[None, :] <= t[:, None]).astype(np.float32)
    cum = np.stack([tril, tril.T])
    tbl = np.zeros((2, 12, L, LANE), np.float32)
    pm = np.zeros((2, 7, L, L), np.float32)
    for lv, g in enumerate((32, 16, 8, 4, 2, 1)):
        a = (t // (2 * g)) * 2 * g
        upper = (t - a) >= g
        same = a[:, None] == a[None, :]
        pm[0, lv] = same & upper[:, None] & (~upper)[None, :]
        pm[1, lv] = same & (~upper)[:, None] & upper[None, :]
        if g >= 4:
            tbl[0, T_SGN + lv] = np.where(upper, 1.0, -1.0)[:, None]
            tbl[1, T_SGN + lv] = np.where(upper, -1.0, 1.0)[:, None]
    pm[:, 6] = np.eye(L)
    r4, r2 = t % 4, t % 2
    tbl[0, T_COEF + 0] = np.isin(r4, (2, 3))[:, None]
    tbl[0, T_COEF + 1] = (r4 == 3)[:, None]
    tbl[0, T_COEF + 2] = (r4 == 0)[:, None]
    tbl[0, T_COEF + 3] = (r2 == 1)[:, None]
    tbl[1, T_COEF + 0] = np.isin(r4, (0, 1))[:, None]
    tbl[1, T_COEF + 1] = (r4 == 3)[:, None]
    tbl[1, T_COEF + 2] = (r4 == 0)[:, None]
    tbl[1, T_COEF + 3] = (r2 == 0)[:, None]
    for p in range(4):
        tbl[:, T_PAIR + p, :, :L] = pm[:, 2 * p]
        if p < 3:
            tbl[:, T_PAIR + p, :, L:] = pm[:, 2 * p + 1]
    return cum, tbl


def _gla_body(qf, kf, vf, gdf, qb, kb, vb, gdb, gu_ref, gbias_ref, cum_ref, tbl_ref,
              of_ref, ob_ref, s_ref, *, nch):
    L = CHUNK
    H = GLA_HEADS
    dk = GLA_DK
    dv = GLA_DV
    scale = dk ** -0.5
    units = [(d, h) for d in range(2) for h in range(H)]

    @pl.when(pl.program_id(1) == 0)
    def _():
        s_ref[...] = jnp.zeros_like(s_ref)

    def chunk(j, carry):
        cjs = (j, nch - 1 - j)
        sts = [pl.multiple_of(cj * L, L) for cj in cjs]
        zk = jnp.zeros((L, dk), bf16)

        las, bs = [], []
        for d, gd_ref in enumerate((gdf, gdb)):
            gd = gd_ref[0, pl.ds(sts[d], L), :].astype(bf16)
            la = _log_sigmoid(_dot(gd, gu_ref[d]) + gbias_ref[d]) * (LOG2E / GLA_TAU)
            hi = la.astype(bf16)
            lo = (la - hi.astype(f32)).astype(bf16)
            las.append(la)
            bs.append(_dot(cum_ref[d], hi) + _dot(cum_ref[d], lo))

        qss, kfs, vs, rs = [], [], [], []
        for d, h in units:
            q_ref, k_ref, v_ref = (qf, kf, vf) if d == 0 else (qb, kb, vb)
            ks = slice(h * dk, (h + 1) * dk)
            la = las[d][:, ks]
            b = bs[d][:, ks]
            k = k_ref[0, pl.ds(sts[d], L), ks]
            qs = q_ref[0, pl.ds(sts[d], L), ks].astype(f32) * scale
            kf32 = k.astype(f32)
            xs = []
            for lv, g in enumerate(GLA_BCAST_LEVELS):
                refs = [a + g - 1 + d for a in range(0, L, 2 * g)]
                bref = jnp.concatenate(
                    [jnp.broadcast_to(b[r:r + 1, :], (2 * g, dk)) for r in refs], axis=0)
                xs.append(jnp.exp2((b - bref) * tbl_ref[d, T_SGN + lv]))
            la_dn = pltpu.roll(la, 1, axis=0)
            la_up = pltpu.roll(la, L - 1, axis=0)
            xs.append(jnp.exp2(la * tbl_ref[d, T_COEF] + la_dn * tbl_ref[d, T_COEF + 1]
                               + la_up * tbl_ref[d, T_COEF + 2]))
            xs.append(jnp.exp2(la * tbl_ref[d, T_COEF + 3]))
            r_u = []
            for p in range(3):
                xa, xb = xs[2 * p], xs[2 * p + 1]
                lhs = jnp.concatenate([(qs * xa).astype(bf16), (qs * xb).astype(bf16)], axis=1)
                rhs = jnp.concatenate(
                    [jnp.concatenate([(kf32 * xa).astype(bf16), zk], axis=1),
                     jnp.concatenate([zk, (kf32 * xb).astype(bf16)], axis=1)], axis=0)
                r_u.append(_dot_nt(lhs, rhs))
            r_u.append(_dot_nt(qs.astype(bf16), jnp.concatenate([k, zk], axis=0)))
            rs.append(r_u)
            qss.append(qs)
            kfs.append(kf32)
            vs.append(v_ref[0, pl.ds(sts[d], L), h * dv:(h + 1) * dv])

        for u, (d, h) in enumerate(units):
            o_ref = of_ref if d == 0 else ob_ref
            b = bs[d][:, h * dk:(h + 1) * dk]
            a2 = jnp.where(tbl_ref[d, T_PAIR + 3] > 0, rs[u][3], 0.0)
            for p in range(2, -1, -1):
                a2 = jnp.where(tbl_ref[d, T_PAIR + p] > 0, rs[u][p], a2)
            v2 = jnp.concatenate([vs[u], vs[u]], axis=0)
            o = _dot(a2.astype(bf16), v2) + _dot_nt((qss[u] * jnp.exp2(b)).astype(bf16),
                                                    s_ref[u].astype(bf16))
            o_ref[0, pl.ds(sts[d], L), h * dv:(h + 1) * dv] = o.astype(o_ref.dtype)

        for u, (d, h) in enumerate(units):
            b = bs[d][:, h * dk:(h + 1) * dk]
            bl = b[L - 1:L, :] if d == 0 else b[0:1, :]
            kt = (kfs[u] * jnp.exp2(bl - b)).astype(bf16)
            s_ref[u] = s_ref[u] * jnp.exp2(bl) + _dot_tn(vs[u], kt)
        return carry

    lax.fori_loop(0, nch, chunk, 0, unroll=True)


def _gla(qk, v, gd, gate_up, gate_b, tb=512):
    B, S, _ = qk.shape
    tb = min(tb, S)
    assert S % tb == 0 and tb % CHUNK == 0
    nb = S // tb
    nch = tb // CHUNK
    R = GLA_GATE_RANK
    gu = jnp.zeros((2, LANE, GLA_KW), f32)
    gu = gu.at[0, :R].set(gate_up[0]).at[1, R:2 * R].set(gate_up[1]).astype(bf16)
    gbias = gate_b.astype(f32).reshape(2, 1, GLA_KW)
    cum_np, tbl_np = _gla_tables()
    cum = jnp.asarray(cum_np, bf16)
    tbl = jnp.asarray(tbl_np, f32)

    def fwd(c):
        return lambda b, i: (b, i, c)

    def bwd(c):
        return lambda b, i: (b, nb - 1 - i, c)

    const3 = lambda b, i: (0, 0, 0)
    in_specs = []
    for mk in (fwd, bwd):
        in_specs += [pl.BlockSpec((1, tb, GLA_KW), mk(0)), pl.BlockSpec((1, tb, GLA_KW), mk(1)),
                     pl.BlockSpec((1, tb, GLA_VW), mk(0)), pl.BlockSpec((1, tb, LANE), mk(0))]
    in_specs += [pl.BlockSpec((2, LANE, GLA_KW), const3), pl.BlockSpec((2, 1, GLA_KW), const3),
                 pl.BlockSpec(cum.shape, const3), pl.BlockSpec(tbl.shape, lambda b, i: (0, 0, 0, 0))]
    return pl.pallas_call(
        functools.partial(_gla_body, nch=nch),
        grid=(B, nb),
        in_specs=in_specs,
        out_specs=[pl.BlockSpec((1, tb, GLA_VW), fwd(0)), pl.BlockSpec((1, tb, GLA_VW), bwd(0))],
        out_shape=[jax.ShapeDtypeStruct((B, S, GLA_VW), bf16)] * 2,
        scratch_shapes=[pltpu.VMEM((2 * GLA_HEADS, GLA_DV, GLA_DK), f32)],
        compiler_params=_cparams(("parallel", "arbitrary")),
        name="gla",
    )(qk, qk, v, gd, qk, qk, v, gd, gu, gbias, cum, tbl)


def _odd_out_body(x_ref, of_ref, ob_ref, r_ref, g_ref, w_ref, gp_ref, out_ref):
    o = of_ref[...].astype(f32) + ob_ref[...].astype(f32)
    hn = jnp.concatenate(
        [_rms(o[:, h * GLA_DV:(h + 1) * GLA_DV]) for h in range(GLA_HEADS)], axis=-1)
    r = r_ref[...].astype(f32)
    y = (hn * g_ref[...]) * (r * _sigmoid(r))
    m = _dot(y.astype(bf16), w_ref[...])
    out_ref[...] = x_ref[...] + _rms(m) * gp_ref[...]


def _odd_out(x2d, of2d, ob2d, r2d, norm_g, w_out, g_post, tm=1024):
    T, Dm = x2d.shape
    row = lambda i: (i, 0)
    const = lambda i: (0, 0)
    deep = dict(pipeline_mode=pl.Buffered(3))
    in_specs = [pl.BlockSpec((tm, Dm), row, **deep), pl.BlockSpec((tm, GLA_VW), row, **deep),
                pl.BlockSpec((tm, GLA_VW), row, **deep), pl.BlockSpec((tm, GLA_VW), row, **deep),
                pl.BlockSpec((1, GLA_VW), const), pl.BlockSpec((GLA_VW, Dm), const),
                pl.BlockSpec((1, Dm), const)]
    out_specs = pl.BlockSpec((tm, Dm), row)

    def outer(*refs):
        pltpu.emit_pipeline(_odd_out_body, grid=(T // tm,), in_specs=in_specs, out_specs=out_specs)(*refs)

    return pl.pallas_call(
        outer,
        in_specs=[pl.BlockSpec(memory_space=pl.ANY)] * len(in_specs),
        out_specs=pl.BlockSpec(memory_space=pl.ANY),
        out_shape=jax.ShapeDtypeStruct((T, Dm), f32),
        compiler_params=pltpu.CompilerParams(vmem_limit_bytes=VMEM_LIMIT),
        name="odd_out",
    )(x2d, of2d, ob2d, r2d, norm_g.reshape(1, GLA_VW), w_out, g_post.reshape(1, Dm))


FFN_HALO = 8
FFN_COLS = 256
GELU_K2 = 2.0 * 0.7978845608028654
GELU_C = 0.044715


def _ffn_body(x_ref, xn_ref, xp_ref, gpre_ref, wu_ref, cw_ref, cb_ref, wd_ref, gpost_ref, p_ref,
              wg_ref, wp_ref, out_ref, lhs_ref, ue_ref, acc_ref, *, tm):
    i = pl.program_id(1)
    keep_prev = (i > 0).astype(f32)
    keep_next = (i < pl.num_programs(1) - 1).astype(f32)
    gpre = gpre_ref[...]
    x = x_ref[0]
    halo = jnp.concatenate([_rms(xn_ref[0]) * gpre * keep_next, _rms(xp_ref[0]) * gpre * keep_prev], axis=0)
    lhs_ref[...] = jnp.concatenate([(_rms(x) * gpre).astype(bf16), halo.astype(bf16)], axis=0)
    acc_ref[...] = jnp.zeros_like(acc_ref)
    te = tm + 2 * FFN_HALO
    nchunks = D_FF // FFN_COLS

    def cols(base, n):
        return pl.ds(pl.multiple_of(base + n * FFN_COLS, FFN_COLS), FFN_COLS)

    def up(n, slot):
        lhs = lhs_ref[...]
        ue_ref[slot, 0] = _dot(lhs, wu_ref[:, cols(D_FF, n)])
        ue_ref[slot, 1] = _dot(lhs, wu_ref[:, cols(0, n)])

    def conv(n, slot, half):
        ue = ue_ref[slot, half]
        cs = cols(D_FF * (1 - half), n)
        w = cw_ref[:, cs]
        above = pltpu.roll(ue, 1, axis=0)[:tm]
        below = pltpu.roll(ue, te - 1, axis=0)[:tm]
        return above * w[0:1] + ue[:tm] * w[1:2] + below * w[2:3] + cb_ref[:, cs]

    def down(n, slot):
        g = conv(n, slot, 0)
        e = jnp.exp2(g * (g * g * (-GELU_K2 * GELU_C * LOG2E) + (-GELU_K2 * LOG2E)))
        act = (g / (1.0 + e) * conv(n, slot, 1)).astype(bf16)
        acc_ref[...] += _dot(act, wd_ref[cols(0, n), :])

    up(0, 0)

    def pair(jj, carry):
        n = 2 * jj
        up(n + 1, 1)
        down(n, 0)
        up(n + 2, 0)
        down(n + 1, 1)
        return carry

    lax.fori_loop(0, (nchunks - 1) // 2, pair, 0)
    down(nchunks - 1, 0)
    x2 = x + _rms(acc_ref[...]) * gpost_ref[...]
    gate = _sigmoid(_dot(x2.astype(bf16), wg_ref[...]))
    out_ref[0] = x2 + _dot(p_ref[0, 0].astype(bf16), wp_ref[...]) * gate


def _resident(shape):
    nd = len(shape)
    return pl.BlockSpec(shape, lambda b, i: (0,) * nd, pipeline_mode=pl.Buffered(1))


def _ffn(x3, g_pre, w_up, conv_w, conv_b, w_down, g_post, p4, layer, w_gate, w_proj, tm=512):
    B, S, Dm = x3.shape
    NF = w_up.shape[1]
    PD = p4.shape[-1]
    tm = min(tm, S)
    nchunks = D_FF // FFN_COLS
    assert S % tm == 0 and tm % FFN_HALO == 0 and D_FF % FFN_COLS == 0 and NF == 2 * D_FF and nchunks % 2 == 1
    hb = tm // FFN_HALO
    nhalo = S // FFN_HALO
    te = tm + 2 * FFN_HALO
    blk = lambda b, i: (b, i, 0)
    return pl.pallas_call(
        functools.partial(_ffn_body, tm=tm),
        grid=(B, S // tm),
        in_specs=[pl.BlockSpec((1, tm, Dm), blk),
                  pl.BlockSpec((1, FFN_HALO, Dm), lambda b, i: (b, jnp.minimum((i + 1) * hb, nhalo - 1), 0)),
                  pl.BlockSpec((1, FFN_HALO, Dm), lambda b, i: (b, jnp.maximum(i * hb - 1, 0), 0)),
                  _resident((1, Dm)), _resident((Dm, NF)), _resident((3, NF)), _resident((1, NF)),
                  _resident((D_FF, Dm)), _resident((1, Dm)),
                  pl.BlockSpec((1, 1, tm, PD), lambda b, i: (layer, b, i, 0)),
                  _resident((Dm, Dm)), _resident((PD, Dm))],
        out_specs=pl.BlockSpec((1, tm, Dm), blk),
        out_shape=jax.ShapeDtypeStruct((B, S, Dm), f32),
        scratch_shapes=[pltpu.VMEM((te, Dm), bf16),
                        pltpu.VMEM((2, 2, te, FFN_COLS), f32),
                        pltpu.VMEM((tm, Dm), f32)],
        compiler_params=_cparams(("parallel", "parallel")),
        name="ffn",
    )(x3, x3, x3, g_pre.reshape(1, Dm), w_up, conv_w, conv_b.reshape(1, NF), w_down,
      g_post.reshape(1, Dm), p4, w_gate, w_proj)


def _pad_cols(w, n):
    return jnp.pad(w, ((0, 0), (0, n - w.shape[1])))


def _trunk(x, p, norm_mix_pre, norm_mix_post, norm_ffn_pre, norm_ffn_post,
           even_w_in, even_na_rpb, even_ml_gate_b, even_ml_norm_g, even_w_out,
           odd_w_in, odd_gate_up, odd_gate_b, odd_norm_g, odd_w_out,
           ffn_w_up, ffn_conv_w, ffn_conv_b, ffn_w_down, ple_w_proj, ple_w_gate):
    B, S, Dm = x.shape
    T = B * S
    depth = norm_mix_pre.shape[0]
    for i in range(depth):
        j = i // 2
        x2d = x.reshape(T, Dm)
        if i % 2 == 0:
            widths = (3 * NA_WIDTH, 4 * ML_WIDTH)
            w_in = even_w_in[j].astype(bf16)
            a3, m4, gt = _norm_matmul(x2d, norm_mix_pre[i], w_in[:, :sum(widths)], widths, (bf16, bf16),
                                      wt=w_in[:, sum(widths):].T)
            y_na = _neighborhood_attention(a3.reshape(B, S, -1), even_na_rpb[j])
            gt = jnp.transpose(gt.reshape(-1, B, S // CHUNK, CHUNK), (1, 2, 0, 3))
            hf, hb = _mlstm(m4.reshape(B, S, -1), gt, even_ml_gate_b[j])
            x2d = _even_out(x2d, y_na.reshape(T, -1), hf.reshape(T, -1), hb.reshape(T, -1), m4,
                            even_ml_norm_g[j], even_w_out[j].astype(bf16), norm_mix_post[i])
        else:
            widths = (2 * GLA_KW, GLA_VW, GLA_VW, LANE)
            w_in = _pad_cols(odd_w_in[j], sum(widths)).astype(bf16)
            qk, v, r, gd = _norm_matmul(x2d, norm_mix_pre[i], w_in, widths, (bf16, bf16, bf16, f32))
            of, ob = _gla(qk.reshape(B, S, -1), v.reshape(B, S, -1), gd.reshape(B, S, LANE),
                          odd_gate_up[j], odd_gate_b[j])
            x2d = _odd_out(x2d, of.reshape(T, -1), ob.reshape(T, -1), r, odd_norm_g[j],
                           odd_w_out[j].astype(bf16), norm_mix_post[i])
        x = _ffn(x2d.reshape(B, S, Dm), norm_ffn_pre[i], ffn_w_up[i].astype(bf16), ffn_conv_w[i],
                 ffn_conv_b[i], ffn_w_down[i].astype(bf16), norm_ffn_post[i], p, i,
                 ple_w_gate[i].astype(bf16), ple_w_proj[i].astype(bf16))
    return x


def kernel(x_prompt, x_sample, p_prompt, p_sample, norm_mix_pre, norm_mix_post, norm_ffn_pre, norm_ffn_post, even_w_in, even_na_rpb, even_ml_gate_b, even_ml_norm_g, even_w_out, odd_w_in, odd_gate_up, odd_gate_b, odd_norm_g, odd_w_out, ffn_w_up, ffn_conv_w, ffn_conv_b, ffn_w_down, ple_w_proj, ple_w_gate):
    params = (norm_mix_pre, norm_mix_post, norm_ffn_pre, norm_ffn_post,
              even_w_in, even_na_rpb, even_ml_gate_b, even_ml_norm_g, even_w_out,
              odd_w_in, odd_gate_up, odd_gate_b, odd_norm_g, odd_w_out,
              ffn_w_up, ffn_conv_w, ffn_conv_b, ffn_w_down, ple_w_proj, ple_w_gate)
    return (_trunk(x_prompt, p_prompt, *params), _trunk(x_sample, p_sample, *params))
```
